```python
import jax, jax.numpy as jnp
from jax import lax
import numpy as np

D_MODEL = 2048
BATCH = 4
SEQ = 2048
DEPTH = 2
DEC_BATCH = 128
DEC_SEQ = 8
PAST_LEN = 16384
PAGE_SIZE = 128

D_CONV = D_MODEL // 2
CONV_A_WIDTH = 31
D_LRU = D_MODEL
LRU_HEADS = 16
LRU_HEAD_DIM = D_LRU // LRU_HEADS
CONV_B_WIDTH = 4
LRU_C = 8.0
D_FF = 3 * D_MODEL
FFN_CONV_WIDTH = 3
D_PLE = 256
LN_EPS = 1e-5
ALPHA = (2.0 * DEPTH) ** 0.25
BETA = (8.0 * DEPTH) ** -0.25
D_IN_TOTAL = 2 * D_CONV + 2 * D_LRU + 2 * D_MODEL
SPLIT_IDX = (D_CONV, 2 * D_CONV, 2 * D_CONV + D_LRU, 2 * D_CONV + 2 * D_LRU, 2 * D_CONV + 2 * D_LRU + D_MODEL)

kernel_name = "hybrid_conformer_rglru_decoder_step"


def layer_norm(x, g, b):
    xf = x.astype(jnp.float32)
    mu = jnp.mean(xf, axis=-1, keepdims=True)
    var = jnp.mean(jnp.square(xf - mu), axis=-1, keepdims=True)
    y = (xf - mu) * lax.rsqrt(var + LN_EPS) * g.astype(jnp.float32) + b.astype(jnp.float32)
    return y.astype(x.dtype)


def causal_dwconv(x, prev, w, b):
    k, c = w.shape
    xp = jnp.concatenate([prev.astype(x.dtype), x], axis=1)
    y = lax.conv_general_dilated(xp, w.astype(x.dtype)[:, None, :], window_strides=(1,), padding="VALID",
                                 dimension_numbers=("NWC", "WIO", "NWC"), feature_group_count=c)
    return y + b.astype(x.dtype), xp[:, xp.shape[1] - (k - 1):]


def _lin_combine(left, right):
    a1, b1 = left
    a2, b2 = right
    return a1 * a2, a2 * b1 + b2


def rg_lru(x, h0, w_r, b_r, w_i, b_i, lam, reset_first):
    bsz, t, c = x.shape
    xh = x.reshape(bsz, t, LRU_HEADS, LRU_HEAD_DIM)
    r = jax.nn.sigmoid(jnp.einsum("bthi,hij->bthj", xh, w_r).reshape(bsz, t, c) + b_r).astype(jnp.float32)
    gi = jax.nn.sigmoid(jnp.einsum("bthi,hij->bthj", xh, w_i).reshape(bsz, t, c) + b_i).astype(jnp.float32)
    log_a = -LRU_C * r * jax.nn.softplus(-lam.astype(jnp.float32))
    a = jnp.exp(log_a)
    mult = jnp.sqrt(-jnp.expm1(2.0 * log_a))
    if reset_first:
        mult = mult.at[:, 0].set(1.0)
    u = mult * gi * x.astype(jnp.float32)
    u = u.at[:, 0].add(a[:, 0] * h0.astype(jnp.float32))
    _, h = lax.associative_scan(_lin_combine, (a, u), axis=1)
    return h.astype(x.dtype), h[:, -1].astype(x.dtype)


def decoder_layer(x, ple, buf_a, buf_b, h0, buf_f, reset_first,
                  w_in, conv_a_w, conv_a_b, ln_a_g, ln_a_b, w_a_out,
                  conv_b_w, conv_b_b, w_r, b_r, w_i, b_i, lru_lambda, w_b_out,
                  w_o, ln1_g, ln1_b, w_up, ffn_conv_w, ffn_conv_b, w_down, ln2_g, ln2_b,
                  w_pe, w_pg, b_pg, ln3_g, ln3_b):
    z = x @ w_in
    a_val, a_gate, b_x, b_gate, g_a, g_b = jnp.split(z, SPLIT_IDX, axis=-1)
    u = a_val * jax.nn.sigmoid(a_gate)
    ca, new_a = causal_dwconv(u, buf_a, conv_a_w, conv_a_b)
    out_a = jax.nn.silu(layer_norm(ca, ln_a_g, ln_a_b)) @ w_a_out
    cb, new_b = causal_dwconv(b_x, buf_b, conv_b_w, conv_b_b)
    hb, h_last = rg_lru(cb, h0, w_r, b_r, w_i, b_i, lru_lambda, reset_first)
    out_b = (hb * jax.nn.gelu(b_gate)) @ w_b_out
    merged = jax.nn.sigmoid(g_a) * out_a + jax.nn.sigmoid(g_b) * out_b
    x = layer_norm(ALPHA * x + merged @ w_o, ln1_g, ln1_b)
    up = x @ w_up
    fu, fg = jnp.split(up, [D_FF], axis=-1)
    fgc, new_f = causal_dwconv(fg, buf_f, ffn_conv_w, ffn_conv_b)
    x = layer_norm(ALPHA * x + (jax.nn.gelu(fgc) * fu) @ w_down, ln2_g, ln2_b)
    e = jax.nn.sigmoid(x @ w_pg + b_pg) * (ple.astype(x.dtype) @ w_pe)
    x = layer_norm(ALPHA * x + e, ln3_g, ln3_b)
    return x, new_a, new_b, h_last, new_f


def setup_inputs(seed: int = 0) -> dict:
    key = jax.random.key(seed)
    ks = iter(jax.random.split(key, 48))
    f32 = jnp.float32
    nrm = lambda shape, s: jax.random.normal(next(ks), shape, f32) * s
    gain = lambda shape: 1.0 + nrm(shape, 0.02)
    L = DEPTH
    s_lam = jax.random.uniform(next(ks), (L, D_LRU), f32, 0.9, 0.999) ** (1.0 / LRU_C)
    lru_lambda = jnp.log(s_lam) - jnp.log1p(-s_lam)
    return {
        "x_prompt": nrm((BATCH, SEQ, D_MODEL), 1.0),
        "x_sample": nrm((DEC_BATCH, DEC_SEQ, D_MODEL), 1.0),
        "state_conv_a": nrm((L, DEC_BATCH, CONV_A_WIDTH - 1, D_CONV), 0.5),
        "state_conv_b": nrm((L, DEC_BATCH, CONV_B_WIDTH - 1, D_LRU), 1.0),
        "state_rglru": nrm((L, DEC_BATCH, D_LRU), 0.5),
        "state_conv_ffn": nrm((L, DEC_BATCH, FFN_CONV_WIDTH - 1, D_FF), 1.0),
        "p_prompt": nrm((L, BATCH, SEQ, D_PLE), 1.0),
        "p_sample": nrm((L, DEC_BATCH, DEC_SEQ, D_PLE), 1.0),
        "w_in": nrm((L, D_MODEL, D_IN_TOTAL), D_MODEL ** -0.5),
        "conv_a_w": nrm((L, CONV_A_WIDTH, D_CONV), CONV_A_WIDTH ** -0.5),
        "conv_a_b": nrm((L, D_CONV), 0.02),
        "ln_a_g": gain((L, D_CONV)),
        "ln_a_b": nrm((L, D_CONV), 0.02),
        "w_a_out": nrm((L, D_CONV, D_MODEL), D_CONV ** -0.5),
        "conv_b_w": nrm((L, CONV_B_WIDTH, D_LRU), CONV_B_WIDTH ** -0.5),
        "conv_b_b": nrm((L, D_LRU), 0.02),
        "w_r": nrm((L, LRU_HEADS, LRU_HEAD_DIM, LRU_HEAD_DIM), LRU_HEAD_DIM ** -0.5),
        "b_r": nrm((L, D_LRU), 0.02),
        "w_i": nrm((L, LRU_HEADS, LRU_HEAD_DIM, LRU_HEAD_DIM), LRU_HEAD_DIM ** -0.5),
        "b_i": nrm((L, D_LRU), 0.02),
        "lru_lambda": lru_lambda,
        "w_b_out": nrm((L, D_LRU, D_MODEL), D_LRU ** -0.5),
        "w_o": nrm((L, D_MODEL, D_MODEL), BETA * D_MODEL ** -0.5),
        "ln1_g": gain((L, D_MODEL)),
        "ln1_b": nrm((L, D_MODEL), 0.02),
        "w_up": nrm((L, D_MODEL, 2 * D_FF), D_MODEL ** -0.5),
        "ffn_conv_w": nrm((L, FFN_CONV_WIDTH, D_FF), FFN_CONV_WIDTH ** -0.5),
        "ffn_conv_b": nrm((L, D_FF), 0.02),
        "w_down": nrm((L, D_FF, D_MODEL), BETA * D_FF ** -0.5),
        "ln2_g": gain((L, D_MODEL)),
        "ln2_b": nrm((L, D_MODEL), 0.02),
        "w_pe": nrm((L, D_PLE, D_MODEL), BETA * D_PLE ** -0.5),
        "w_pg": nrm((L, D_MODEL, D_MODEL), D_MODEL ** -0.5),
        "b_pg": nrm((L, D_MODEL), 0.02),
        "ln3_g": gain((L, D_MODEL)),
        "ln3_b": nrm((L, D_MODEL), 0.02),
    }


def reference(x_prompt, x_sample, state_conv_a, state_conv_b, state_rglru, state_conv_ffn, p_prompt, p_sample,
              w_in, conv_a_w, conv_a_b, ln_a_g, ln_a_b, w_a_out, conv_b_w, conv_b_b, w_r, b_r, w_i, b_i,
              lru_lambda, w_b_out, w_o, ln1_g, ln1_b, w_up, ffn_conv_w, ffn_conv_b, w_down, ln2_g, ln2_b,
              w_pe, w_pg, b_pg, ln3_g, ln3_b):
    dt = x_prompt.dtype
    xp, xs = x_prompt, x_sample
    pa, pb, ph, pf = [], [], [], []
    sa, sb, sh, sf = [], [], [], []
    for l in range(DEPTH):
        wl = (w_in[l], conv_a_w[l], conv_a_b[l], ln_a_g[l], ln_a_b[l], w_a_out[l],
              conv_b_w[l], conv_b_b[l], w_r[l], b_r[l], w_i[l], b_i[l], lru_lambda[l], w_b_out[l],
              w_o[l], ln1_g[l], ln1_b[l], w_up[l], ffn_conv_w[l], ffn_conv_b[l], w_down[l], ln2_g[l], ln2_b[l],
              w_pe[l], w_pg[l], b_pg[l], ln3_g[l], ln3_b[l])
        bsz = xp.shape[0]
        xp, na, nb, nh, nf = decoder_layer(
            xp, p_prompt[l],
            jnp.zeros((bsz, CONV_A_WIDTH - 1, D_CONV), dt), jnp.zeros((bsz, CONV_B_WIDTH - 1, D_LRU), dt),
            jnp.zeros((bsz, D_LRU), dt), jnp.zeros((bsz, FFN_CONV_WIDTH - 1, D_FF), dt), True, *wl)
        pa.append(na); pb.append(nb); ph.append(nh); pf.append(nf)
        xs, na, nb, nh, nf = decoder_layer(
            xs, p_sample[l], state_conv_a[l], state_conv_b[l], state_rglru[l], state_conv_ffn[l], False, *wl)
        sa.append(na); sb.append(nb); sh.append(nh); sf.append(nf)
    return (xp, xs,
            jnp.stack(pa), jnp.stack(pb), jnp.stack(ph), jnp.stack(pf),
            jnp.stack(sa), jnp.stack(sb), jnp.stack(sh), jnp.stack(sf))
```

```python
import functools

import jax
import jax.numpy as jnp
from jax import lax
from jax.experimental import pallas as pl
from jax.experimental.pallas import tpu as pltpu

F32 = jnp.float32
BF16 = jnp.bfloat16

D_MODEL = 2048
DEPTH = 2
D_CONV = D_MODEL // 2
CONV_A_WIDTH = 31
D_LRU = D_MODEL
LRU_HEADS = 16
LRU_HEAD_DIM = D_LRU // LRU_HEADS
CONV_B_WIDTH = 4
LRU_C = 8.0
D_FF = 3 * D_MODEL
FFN_CONV_WIDTH = 3
D_PLE = 256
LN_EPS = 1e-5
ALPHA = (2.0 * DEPTH) ** 0.25
OFF_AV, OFF_AG, OFF_BX, OFF_BG, OFF_GA, OFF_GB = 0, D_CONV, 2 * D_CONV, 2 * D_CONV + D_LRU, 2 * D_CONV + 2 * D_LRU, 2 * D_CONV + 2 * D_LRU + D_MODEL

SUBLANES = 8
VMEM_LIMIT_BYTES = 56 * 1024 * 1024

BM_PROMPT = 512
BM_SAMPLE = 256
CHUNK = 512
CM = 256
CF = 512
RB = 64
LN_RB = 32
LC = 256
HIST_A = 32
HIST = SUBLANES


def _dot(a, b):
    return jnp.dot(a, b, preferred_element_type=F32)


def _ln(v, g, b):
    mu = jnp.mean(v, axis=-1, keepdims=True)
    c = v - mu
    var = jnp.mean(c * c, axis=-1, keepdims=True)
    return c * lax.rsqrt(var + LN_EPS) * g + b


def _residual_ln_rows(x_ref, acc_ref, g_ref, b_ref, o_ref, bm):
    def body(r, carry):
        rows = pl.ds(pl.multiple_of(r * LN_RB, LN_RB), LN_RB)
        v = ALPHA * x_ref[rows, :] + acc_ref[rows, :]
        o_ref[rows, :] = _ln(v, g_ref[...], b_ref[...])
        return carry
    lax.fori_loop(0, bm // LN_RB, body, 0)


def _scan8(a, u):
    rows = lax.broadcasted_iota(jnp.int32, a.shape, 0) & (SUBLANES - 1)
    for d in (1, 2, 4):
        m = rows >= d
        a_sh = pltpu.roll(a, d, 0)
        u_sh = pltpu.roll(u, d, 0)
        u = jnp.where(m, a * u_sh + u, u)
        a = jnp.where(m, a * a_sh, a)
    return a, u


def _lru_gates(cb, wr_ref, br_ref, wi_ref, bi_ref, lam_ref, c, first_tile):
    hpc = CHUNK // LRU_HEAD_DIM
    rs, gs = [], []
    for hh in range(hpc):
        cbh = cb[:, hh * LRU_HEAD_DIM:(hh + 1) * LRU_HEAD_DIM].astype(BF16)
        rs.append(_dot(cbh, wr_ref[c * hpc + hh]))
        gs.append(_dot(cbh, wi_ref[c * hpc + hh]))
    cols = slice(c * CHUNK, (c + 1) * CHUNK)
    r = jax.nn.sigmoid(jnp.concatenate(rs, axis=-1) + br_ref[:, cols])
    gi = jax.nn.sigmoid(jnp.concatenate(gs, axis=-1) + bi_ref[:, cols])
    log_a = (-LRU_C) * r * jax.nn.softplus(-lam_ref[:, cols])
    a = jnp.exp(log_a)
    th = jnp.tanh(log_a)
    mult = jnp.sqrt(-2.0 * th / (1.0 - th))
    if first_tile is not None:
        row0 = lax.broadcasted_iota(jnp.int32, mult.shape, 0) == 0
        mult = jnp.where(jnp.logical_and(row0, first_tile), 1.0, mult)
    return a, mult * gi * cb


def _conv31_prompt(u_scr, cw_ref, cb_ref, ca_scr, bm):
    off0 = HIST_A - (CONV_A_WIDTH - 1)
    def body(r, carry):
        base = pl.multiple_of(r * RB, RB)
        for lc in range(D_CONV // LC):
            lanes = pl.ds(lc * LC, LC)
            win = u_scr.at[pl.ds(base, RB + HIST_A), lanes]
            acc = jnp.broadcast_to(cb_ref[:, lanes], (RB, LC))
            for k in range(CONV_A_WIDTH):
                acc = acc + cw_ref[k:k + 1, lanes] * win[pl.ds(off0 + k, RB), :]
            ca_scr[pl.ds(base, RB), lanes] = acc
        return carry
    lax.fori_loop(0, bm // RB, body, 0)


def _conv31_sample(ext_scr, cw_ref, cb_ref, ca_scr, bm):
    sb_n = RB // SUBLANES
    def body(r, carry):
        s0 = pl.multiple_of(r * sb_n, sb_n)
        for lc in range(D_CONV // LC):
            lanes = pl.ds(lc * LC, LC)
            acc = jnp.broadcast_to(cb_ref[:, lanes][None], (sb_n, SUBLANES, LC))
            for k in range(CONV_A_WIDTH):
                acc = acc + cw_ref[k:k + 1, lanes][None] * ext_scr[pl.ds(s0, sb_n), k:k + SUBLANES, lanes]
            ca_scr[pl.ds(pl.multiple_of(r * RB, RB), RB), lanes] = acc.reshape(RB, LC)
        return carry
    lax.fori_loop(0, bm // RB, body, 0)


def _ln_silu_rows(ca_scr, lg_ref, lb_ref, ya_ref, bm):
    def body(r, carry):
        rows = pl.ds(pl.multiple_of(r * RB, RB), RB)
        y = _ln(ca_scr[rows, :], lg_ref[...], lb_ref[...])
        ya_ref[rows, :] = (y * jax.nn.sigmoid(y)).astype(BF16)
        return carry
    lax.fori_loop(0, bm // RB, body, 0)


def _a1_prompt_kernel(x_ref, wv_ref, wg_ref, cw_ref, cb_ref, lg_ref, lb_ref, ya_ref, na_ref, u_scr, ca_scr, *, bm, tps):
    t = pl.program_id(0) % tps

    @pl.when(t == 0)
    def _():
        u_scr[0:HIST_A, :] = jnp.zeros((HIST_A, D_CONV), F32)

    @pl.when(t != 0)
    def _():
        u_scr[0:HIST_A, :] = u_scr[bm:bm + HIST_A, :]

    xb = x_ref[...].astype(BF16)
    for c in range(D_CONV // CHUNK):
        cols = slice(c * CHUNK, (c + 1) * CHUNK)
        av = _dot(xb, wv_ref[:, cols])
        ag = _dot(xb, wg_ref[:, cols])
        u_scr[HIST_A:HIST_A + bm, cols] = av * jax.nn.sigmoid(ag)
    na_ref[...] = u_scr[HIST_A + bm - (CONV_A_WIDTH - 1):HIST_A + bm, :]
    _conv31_prompt(u_scr, cw_ref, cb_ref, ca_scr, bm)
    _ln_silu_rows(ca_scr, lg_ref, lb_ref, ya_ref, bm)


def _a1_sample_kernel(x_ref, st_ref, wv_ref, wg_ref, cw_ref, cb_ref, lg_ref, lb_ref, ya_ref, na_ref, ext_scr, ca_scr, *, bm):
    s_n = bm // SUBLANES
    hist = CONV_A_WIDTH - 1
    xb = x_ref[...].astype(BF16)
    ext_scr[:, 0:hist, :] = st_ref[...]
    for c in range(D_CONV // CHUNK):
        cols = slice(c * CHUNK, (c + 1) * CHUNK)
        av = _dot(xb, wv_ref[:, cols])
        ag = _dot(xb, wg_ref[:, cols])
        ext_scr[:, hist:hist + SUBLANES, cols] = (av * jax.nn.sigmoid(ag)).reshape(s_n, SUBLANES, CHUNK)
    na_ref[...] = ext_scr[:, SUBLANES:SUBLANES + hist, :]
    _conv31_sample(ext_scr, cw_ref, cb_ref, ca_scr, bm)
    _ln_silu_rows(ca_scr, lg_ref, lb_ref, ya_ref, bm)


def _a2_prompt_kernel(x_ref, wbx_ref, wbg_ref, cw_ref, cb_ref, wr_ref, br_ref, wi_ref, bi_ref, lam_ref,
                      yb_ref, nb_ref, hl_ref, bx_scr, a_scr, u_scr, hc_scr, *, bm, tps):
    t = pl.program_id(0) % tps
    hist = CONV_B_WIDTH - 1

    @pl.when(t == 0)
    def _():
        bx_scr[0:HIST, :] = jnp.zeros((HIST, D_LRU), F32)
        hc_scr[...] = jnp.zeros((SUBLANES, D_LRU), F32)

    @pl.when(t != 0)
    def _():
        bx_scr[0:HIST, :] = bx_scr[bm:bm + HIST, :]

    xb = x_ref[...].astype(BF16)
    first_tile = t == 0
    for c in range(D_LRU // CHUNK):
        cols = slice(c * CHUNK, (c + 1) * CHUNK)
        bx = _dot(xb, wbx_ref[:, cols])
        bx_scr[HIST:HIST + bm, cols] = bx
        cb = cb_ref[:, cols] + cw_ref[hist:hist + 1, cols] * bx
        for k in range(hist):
            cb = cb + cw_ref[k:k + 1, cols] * bx_scr[HIST - hist + k:HIST - hist + k + bm, cols]
        a, u = _lru_gates(cb, wr_ref, br_ref, wi_ref, bi_ref, lam_ref, c, first_tile)
        a, u = _scan8(a, u)
        a_scr[:, cols] = a
        u_scr[:, cols] = u
    nb_ref[...] = bx_scr[HIST + bm - hist:HIST + bm, :]

    def body(r, carry):
        rows = pl.ds(pl.multiple_of(r * SUBLANES, SUBLANES), SUBLANES)
        h = u_scr[rows, :] + a_scr[rows, :] * carry
        u_scr[rows, :] = h
        return jnp.broadcast_to(h[SUBLANES - 1:SUBLANES, :], (SUBLANES, D_LRU))
    carry = lax.fori_loop(0, bm // SUBLANES, body, hc_scr[...])
    hc_scr[...] = carry
    hl_ref[...] = carry[0:1, :]

    for c in range(D_LRU // CHUNK):
        cols = slice(c * CHUNK, (c + 1) * CHUNK)
        bg = _dot(xb, wbg_ref[:, cols])
        yb_ref[:, cols] = (u_scr[:, cols] * jax.nn.gelu(bg)).astype(BF16)


def _a2_sample_kernel(x_ref, stb_ref, h0_ref, wbx_ref, wbg_ref, cw_ref, cb_ref, wr_ref, br_ref, wi_ref, bi_ref, lam_ref,
                      yb_ref, nb_ref, hl_ref, ext_scr, *, bm):
    s_n = bm // SUBLANES
    hist = CONV_B_WIDTH - 1
    xb = x_ref[...].astype(BF16)
    ext_scr[:, HIST - hist:HIST, :] = stb_ref[...]
    for c in range(D_LRU // CHUNK):
        cols = slice(c * CHUNK, (c + 1) * CHUNK)
        bx = _dot(xb, wbx_ref[:, cols])
        ext_scr[:, HIST:HIST + SUBLANES, cols] = bx.reshape(s_n, SUBLANES, CHUNK)
        cb3 = cb_ref[:, cols][None] + cw_ref[hist:hist + 1, cols][None] * bx.reshape(s_n, SUBLANES, CHUNK)
        for k in range(hist):
            cb3 = cb3 + cw_ref[k:k + 1, cols][None] * ext_scr[:, HIST - hist + k:HIST - hist + k + SUBLANES, cols]
        cb = cb3.reshape(bm, CHUNK)
        a, u = _lru_gates(cb, wr_ref, br_ref, wi_ref, bi_ref, lam_ref, c, None)
        a, u = _scan8(a, u)
        h3 = u.reshape(s_n, SUBLANES, CHUNK) + a.reshape(s_n, SUBLANES, CHUNK) * h0_ref[:, cols][:, None, :]
        hl_ref[:, cols] = h3[:, SUBLANES - 1, :]
        bg = _dot(xb, wbg_ref[:, cols])
        yb_ref[:, cols] = (h3.reshape(bm, CHUNK) * jax.nn.gelu(bg)).astype(BF16)
    nb_ref[...] = ext_scr[:, HIST + SUBLANES - hist:HIST + SUBLANES, :]


def _merge_kernel(x_ref, ya_ref, yb_ref, wga_ref, wgb_ref, wa_ref, wb_ref, wo_ref, g_ref, b_ref, o_ref, xb_scr, acc_scr, *, bm, nc):
    c = pl.program_id(1)

    @pl.when(c == 0)
    def _():
        xb_scr[...] = x_ref[...].astype(BF16)
        acc_scr[...] = jnp.zeros((bm, D_MODEL), F32)

    xb = xb_scr[...]
    out_a = _dot(ya_ref[...], wa_ref[...])
    out_b = _dot(yb_ref[...], wb_ref[...])
    merged = jax.nn.sigmoid(_dot(xb, wga_ref[...])) * out_a + jax.nn.sigmoid(_dot(xb, wgb_ref[...])) * out_b
    acc_scr[...] += _dot(merged.astype(BF16), wo_ref[...])

    @pl.when(c == nc - 1)
    def _():
        _residual_ln_rows(x_ref, acc_scr, g_ref, b_ref, o_ref, bm)


def _ffn_tail(x_ref, fu, fgc, wd_ref, g_ref, b_ref, o_ref, acc_scr, c, nc, bm):
    hf = jax.nn.gelu(fgc) * fu
    acc_scr[...] += _dot(hf.astype(BF16), wd_ref[...])

    @pl.when(c == nc - 1)
    def _():
        _residual_ln_rows(x_ref, acc_scr, g_ref, b_ref, o_ref, bm)


def _ffn_prompt_kernel(x_ref, wfu_ref, wfg_ref, cw_ref, cb_ref, wd_ref, g_ref, b_ref, o_ref, nf_ref,
                       xb_scr, acc_scr, fg_scr, carry_scr, *, bm, tps, nc):
    t = pl.program_id(0) % tps
    c = pl.program_id(1)
    hist = FFN_CONV_WIDTH - 1

    @pl.when(c == 0)
    def _():
        xb_scr[...] = x_ref[...].astype(BF16)
        acc_scr[...] = jnp.zeros((bm, D_MODEL), F32)

    xb = xb_scr[...]
    fu = _dot(xb, wfu_ref[...])
    fg = _dot(xb, wfg_ref[...])

    @pl.when(t == 0)
    def _():
        fg_scr[0:HIST, :] = jnp.zeros((HIST, CF), F32)

    @pl.when(t != 0)
    def _():
        fg_scr[0:HIST, :] = carry_scr[c]

    fg_scr[HIST:HIST + bm, :] = fg
    carry_scr[c] = fg[bm - HIST:bm, :]
    nf_ref[...] = fg[bm - hist:bm, :]
    fgc = cb_ref[...] + cw_ref[hist:hist + 1, :] * fg
    for k in range(hist):
        fgc = fgc + cw_ref[k:k + 1, :] * fg_scr[HIST - hist + k:HIST - hist + k + bm, :]
    _ffn_tail(x_ref, fu, fgc, wd_ref, g_ref, b_ref, o_ref, acc_scr, c, nc, bm)


def _ffn_sample_kernel(x_ref, st_ref, wfu_ref, wfg_ref, cw_ref, cb_ref, wd_ref, g_ref, b_ref, o_ref, nf_ref,
                       xb_scr, acc_scr, ext_scr, *, bm, nc):
    c = pl.program_id(1)
    s_n = bm // SUBLANES
    hist = FFN_CONV_WIDTH - 1

    @pl.when(c == 0)
    def _():
        xb_scr[...] = x_ref[...].astype(BF16)
        acc_scr[...] = jnp.zeros((bm, D_MODEL), F32)

    xb = xb_scr[...]
    fu = _dot(xb, wfu_ref[...])
    fg3 = _dot(xb, wfg_ref[...]).reshape(s_n, SUBLANES, CF)
    ext_scr[:, HIST - hist:HIST, :] = st_ref[...]
    ext_scr[:, HIST:HIST + SUBLANES, :] = fg3
    nf_ref[...] = ext_scr[:, HIST + SUBLANES - hist:HIST + SUBLANES, :]
    fgc3 = cb_ref[...][None] + cw_ref[hist:hist + 1, :][None] * fg3
    for k in range(hist):
        fgc3 = fgc3 + cw_ref[k:k + 1, :][None] * ext_scr[:, HIST - hist + k:HIST - hist + k + SUBLANES, :]
    _ffn_tail(x_ref, fu, fgc3.reshape(bm, CF), wd_ref, g_ref, b_ref, o_ref, acc_scr, c, nc, bm)


def _ple_kernel(x_ref, p_ref, wpg_ref, bpg_ref, wpe_ref, g_ref, b_ref, o_ref, e_scr, *, bm):
    xb = x_ref[...].astype(BF16)
    pb = p_ref[...].astype(BF16)
    for c in range(D_MODEL // CHUNK):
        cols = slice(c * CHUNK, (c + 1) * CHUNK)
        gate = jax.nn.sigmoid(_dot(xb, wpg_ref[:, cols]) + bpg_ref[:, cols])
        e_scr[:, cols] = gate * _dot(pb, wpe_ref[:, cols])
    _residual_ln_rows(x_ref, e_scr, g_ref, b_ref, o_ref, bm)


def _params(n_axes):
    return pltpu.CompilerParams(dimension_semantics=("arbitrary",) * n_axes, vmem_limit_bytes=VMEM_LIMIT_BYTES)


def _resident(shape, index_map):
    return pl.BlockSpec(shape, index_map, pipeline_mode=pl.Buffered(1))


def _row(l, width):
    return _resident((None, 1, width), lambda *idx: (l, 0, 0))


def _mixer_a(l, x2, state, w_in_b, conv_w, conv_b, ln_g, ln_b, prompt, seqs):
    n = x2.shape[0]
    bm = BM_PROMPT if prompt else BM_SAMPLE
    hist = CONV_A_WIDTH - 1
    weights = [
        _resident((None, D_MODEL, D_CONV), lambda i: (l, 0, OFF_AV // D_CONV)),
        _resident((None, D_MODEL, D_CONV), lambda i: (l, 0, OFF_AG // D_CONV)),
        _resident((None, CONV_A_WIDTH, D_CONV), lambda i: (l, 0, 0)),
        _row(l, D_CONV), _row(l, D_CONV), _row(l, D_CONV),
    ]
    x_spec = pl.BlockSpec((bm, D_MODEL), lambda i: (i, 0))
    ya_spec = pl.BlockSpec((bm, D_CONV), lambda i: (i, 0))
    out_shape = [jax.ShapeDtypeStruct((n, D_CONV), BF16), jax.ShapeDtypeStruct((seqs, hist, D_CONV), F32)]
    args = (w_in_b, w_in_b, conv_w, conv_b, ln_g, ln_b)
    if prompt:
        tps = (n // seqs) // bm
        return pl.pallas_call(
            functools.partial(_a1_prompt_kernel, bm=bm, tps=tps),
            grid=(n // bm,),
            in_specs=[x_spec] + weights,
            out_specs=[ya_spec, pl.BlockSpec((None, hist, D_CONV), lambda i: (i // tps, 0, 0))],
            out_shape=out_shape,
            scratch_shapes=[pltpu.VMEM((HIST_A + bm, D_CONV), F32), pltpu.VMEM((bm, D_CONV), F32)],
            compiler_params=_params(1), name=f"mixer_a_prompt_l{l}",
        )(x2, *args)
    s_n = bm // SUBLANES
    return pl.pallas_call(
        functools.partial(_a1_sample_kernel, bm=bm),
        grid=(n // bm,),
        in_specs=[x_spec, pl.BlockSpec((None, s_n, hist, D_CONV), lambda i: (l, i, 0, 0))] + weights,
        out_specs=[ya_spec, pl.BlockSpec((s_n, hist, D_CONV), lambda i: (i, 0, 0))],
        out_shape=out_shape,
        scratch_shapes=[pltpu.VMEM((s_n, hist + SUBLANES + 2, D_CONV), F32), pltpu.VMEM((bm, D_CONV), F32)],
        compiler_params=_params(1), name=f"mixer_a_sample_l{l}",
    )(x2, state, *args)


def _mixer_b(l, x2, state_b, state_h, w_in_b, conv_w, conv_b, w_r_b, b_r, w_i_b, b_i, lam, prompt, seqs):
    n = x2.shape[0]
    bm = BM_PROMPT if prompt else BM_SAMPLE
    hist = CONV_B_WIDTH - 1
    gate_w = _resident((None, LRU_HEADS, LRU_HEAD_DIM, LRU_HEAD_DIM), lambda i: (l, 0, 0, 0))
    weights = [
        _resident((None, D_MODEL, D_LRU), lambda i: (l, 0, OFF_BX // D_LRU)),
        _resident((None, D_MODEL, D_LRU), lambda i: (l, 0, OFF_BG // D_LRU)),
        _resident((None, CONV_B_WIDTH, D_LRU), lambda i: (l, 0, 0)),
        _row(l, D_LRU), gate_w, _row(l, D_LRU), gate_w, _row(l, D_LRU), _row(l, D_LRU),
    ]
    x_spec = pl.BlockSpec((bm, D_MODEL), lambda i: (i, 0))
    yb_spec = pl.BlockSpec((bm, D_LRU), lambda i: (i, 0))
    args = (w_in_b, w_in_b, conv_w, conv_b, w_r_b, b_r, w_i_b, b_i, lam)
    if prompt:
        tps = (n // seqs) // bm
        yb, nb, hl = pl.pallas_call(
            functools.partial(_a2_prompt_kernel, bm=bm, tps=tps),
            grid=(n // bm,),
            in_specs=[x_spec] + weights,
            out_specs=[yb_spec,
                       pl.BlockSpec((None, hist, D_LRU), lambda i: (i // tps, 0, 0)),
                       pl.BlockSpec((None, 1, D_LRU), lambda i: (i // tps, 0, 0))],
            out_shape=[jax.ShapeDtypeStruct((n, D_LRU), BF16),
                       jax.ShapeDtypeStruct((seqs, hist, D_LRU), F32),
                       jax.ShapeDtypeStruct((seqs, 1, D_LRU), F32)],
            scratch_shapes=[pltpu.VMEM((HIST + bm, D_LRU), F32), pltpu.VMEM((bm, D_LRU), F32),
                            pltpu.VMEM((bm, D_LRU), F32), pltpu.VMEM((SUBLANES, D_LRU), F32)],
            compiler_params=_params(1), name=f"mixer_b_prompt_l{l}",
        )(x2, *args)
        return yb, nb, hl.reshape(seqs, D_LRU)
    s_n = bm // SUBLANES
    return pl.pallas_call(
        functools.partial(_a2_sample_kernel, bm=bm),
        grid=(n // bm,),
        in_specs=[x_spec,
                  pl.BlockSpec((None, s_n, hist, D_LRU), lambda i: (l, i, 0, 0)),
                  pl.BlockSpec((None, s_n, D_LRU), lambda i: (l, i, 0))] + weights,
        out_specs=[yb_spec,
                   pl.BlockSpec((s_n, hist, D_LRU), lambda i: (i, 0, 0)),
                   pl.BlockSpec((s_n, D_LRU), lambda i: (i, 0))],
        out_shape=[jax.ShapeDtypeStruct((n, D_LRU), BF16),
                   jax.ShapeDtypeStruct((seqs, hist, D_LRU), F32),
                   jax.ShapeDtypeStruct((seqs, D_LRU), F32)],
        scratch_shapes=[pltpu.VMEM((s_n, HIST + SUBLANES, D_LRU), F32)],
        compiler_params=_params(1), name=f"mixer_b_sample_l{l}",
    )(x2, state_b, state_h, *args)


def _merge(l, x2, ya, yb, w_in_b, w_a_out_b, w_b_out_b, w_o_b, g, b, bm):
    n = x2.shape[0]
    nc = D_MODEL // CM
    return pl.pallas_call(
        functools.partial(_merge_kernel, bm=bm, nc=nc),
        grid=(n // bm, nc),
        in_specs=[
            pl.BlockSpec((bm, D_MODEL), lambda i, c: (i, 0)),
            pl.BlockSpec((bm, D_CONV), lambda i, c: (i, 0)),
            pl.BlockSpec((bm, D_LRU), lambda i, c: (i, 0)),
            pl.BlockSpec((None, D_MODEL, CM), lambda i, c: (l, 0, OFF_GA // CM + c)),
            pl.BlockSpec((None, D_MODEL, CM), lambda i, c: (l, 0, OFF_GB // CM + c)),
            pl.BlockSpec((None, D_CONV, CM), lambda i, c: (l, 0, c)),
            pl.BlockSpec((None, D_LRU, CM), lambda i, c: (l, 0, c)),
            pl.BlockSpec((None, CM, D_MODEL), lambda i, c: (l, c, 0)),
            _row(l, D_MODEL), _row(l, D_MODEL),
        ],
        out_specs=pl.BlockSpec((bm, D_MODEL), lambda i, c: (i, 0)),
        out_shape=jax.ShapeDtypeStruct((n, D_MODEL), F32),
        scratch_shapes=[pltpu.VMEM((bm, D_MODEL), BF16), pltpu.VMEM((bm, D_MODEL), F32)],
        compiler_params=_params(2), name=f"merge_l{l}_{bm}",
    )(x2, ya, yb, w_in_b, w_in_b, w_a_out_b, w_b_out_b, w_o_b, g, b)


def _ffn(l, x2, state, w_up_b, conv_w, conv_b, w_down_b, g, b, prompt, seqs):
    n = x2.shape[0]
    bm = BM_PROMPT if prompt else BM_SAMPLE
    nc = D_FF // CF
    hist = FFN_CONV_WIDTH - 1
    weights = [
        pl.BlockSpec((None, D_MODEL, CF), lambda i, c: (l, 0, c)),
        pl.BlockSpec((None, D_MODEL, CF), lambda i, c: (l, 0, nc + c)),
        pl.BlockSpec((None, FFN_CONV_WIDTH, CF), lambda i, c: (l, 0, c)),
        pl.BlockSpec((None, 1, CF), lambda i, c: (l, 0, c)),
        pl.BlockSpec((None, CF, D_MODEL), lambda i, c: (l, c, 0)),
        _row(l, D_MODEL), _row(l, D_MODEL),
    ]
    x_spec = pl.BlockSpec((bm, D_MODEL), lambda i, c: (i, 0))
    o_spec = pl.BlockSpec((bm, D_MODEL), lambda i, c: (i, 0))
    x_shape = jax.ShapeDtypeStruct((n, D_MODEL), F32)
    scratch = [pltpu.VMEM((bm, D_MODEL), BF16), pltpu.VMEM((bm, D_MODEL), F32)]
    args = (w_up_b, w_up_b, conv_w, conv_b, w_down_b, g, b)
    if prompt:
        tps = (n // seqs) // bm
        x_out, nf_tiles = pl.pallas_call(
            functools.partial(_ffn_prompt_kernel, bm=bm, tps=tps, nc=nc),
            grid=(n // bm, nc),
            in_specs=[x_spec] + weights,
            out_specs=[o_spec, pl.BlockSpec((None, hist, CF), lambda i, c: (i, 0, c))],
            out_shape=[x_shape, jax.ShapeDtypeStruct((n // bm, hist, D_FF), F32)],
            scratch_shapes=scratch + [pltpu.VMEM((HIST + bm, CF), F32), pltpu.VMEM((nc, HIST, CF), F32)],
            compiler_params=_params(2), name=f"ffn_prompt_l{l}",
        )(x2, *args)
        return x_out, nf_tiles.reshape(seqs, tps, hist, D_FF)[:, tps - 1]
    s_n = bm // SUBLANES
    return pl.pallas_call(
        functools.partial(_ffn_sample_kernel, bm=bm, nc=nc),
        grid=(n // bm, nc),
        in_specs=[x_spec, pl.BlockSpec((None, s_n, hist, CF), lambda i, c: (l, i, 0, c))] + weights,
        out_specs=[o_spec, pl.BlockSpec((s_n, hist, CF), lambda i, c: (i, 0, c))],
        out_shape=[x_shape, jax.ShapeDtypeStruct((seqs, hist, D_FF), F32)],
        scratch_shapes=scratch + [pltpu.VMEM((s_n, HIST + SUBLANES, CF), F32)],
        compiler_params=_params(2), name=f"ffn_sample_l{l}",
    )(x2, state, *args)


def _ple(l, x2, p2, w_pg_b, b_pg, w_pe_b, g, b, bm):
    n = x2.shape[0]
    return pl.pallas_call(
        functools.partial(_ple_kernel, bm=bm),
        grid=(n // bm,),
        in_specs=[
            pl.BlockSpec((bm, D_MODEL), lambda i: (i, 0)),
            pl.BlockSpec((None, bm, D_PLE), lambda i: (l, i, 0)),
            _resident((None, D_MODEL, D_MODEL), lambda i: (l, 0, 0)),
            _row(l, D_MODEL),
            _resident((None, D_PLE, D_MODEL), lambda i: (l, 0, 0)),
            _row(l, D_MODEL), _row(l, D_MODEL),
        ],
        out_specs=pl.BlockSpec((bm, D_MODEL), lambda i: (i, 0)),
        out_shape=jax.ShapeDtypeStruct((n, D_MODEL), F32),
        scratch_shapes=[pltpu.VMEM((bm, D_MODEL), F32)],
        compiler_params=_params(1), name=f"ple_l{l}_{bm}",
    )(x2, p2, w_pg_b, b_pg, w_pe_b, g, b)


def kernel(x_prompt, x_sample, state_conv_a, state_conv_b, state_rglru, state_conv_ffn, p_prompt, p_sample, w_in, conv_a_w, conv_a_b, ln_a_g, ln_a_b, w_a_out, conv_b_w, conv_b_b, w_r, b_r, w_i, b_i, lru_lambda, w_b_out, w_o, ln1_g, ln1_b, w_up, ffn_conv_w, ffn_conv_b, w_down, ln2_g, ln2_b, w_pe, w_pg, b_pg, ln3_g, ln3_b):
    bp, tp, _ = x_prompt.shape
    bs, ts, _ = x_sample.shape
    assert ts == SUBLANES and tp % BM_PROMPT == 0 and (bs * ts) % BM_SAMPLE == 0
    assert w_in.shape == (DEPTH, D_MODEL, OFF_GB + D_MODEL)

    w_in_b, w_a_out_b, w_b_out_b, w_o_b = (w.astype(BF16) for w in (w_in, w_a_out, w_b_out, w_o))
    w_up_b, w_down_b, w_pe_b, w_pg_b = (w.astype(BF16) for w in (w_up, w_down, w_pe, w_pg))
    w_r_b, w_i_b = w_r.astype(BF16), w_i.astype(BF16)
    row = lambda v: v.reshape(DEPTH, 1, v.shape[-1])
    conv_a_b, ln_a_g, ln_a_b, conv_b_b, b_r, b_i, lru_lambda = map(row, (conv_a_b, ln_a_g, ln_a_b, conv_b_b, b_r, b_i, lru_lambda))
    ln1_g, ln1_b, ffn_conv_b, ln2_g, ln2_b, b_pg, ln3_g, ln3_b = map(row, (ln1_g, ln1_b, ffn_conv_b, ln2_g, ln2_b, b_pg, ln3_g, ln3_b))

    groups = [
        dict(x=x_prompt.reshape(bp * tp, D_MODEL), p=p_prompt.reshape(DEPTH, bp * tp, D_PLE), prompt=True, seqs=bp, bm=BM_PROMPT),
        dict(x=x_sample.reshape(bs * ts, D_MODEL), p=p_sample.reshape(DEPTH, bs * ts, D_PLE), prompt=False, seqs=bs, bm=BM_SAMPLE),
    ]
    states = [[], []]
    for l in range(DEPTH):
        for gi, grp in enumerate(groups):
            x2, prompt, seqs, bm = grp["x"], grp["prompt"], grp["seqs"], grp["bm"]
            ya, new_a = _mixer_a(l, x2, state_conv_a, w_in_b, conv_a_w, conv_a_b, ln_a_g, ln_a_b, prompt, seqs)
            yb, new_b, h_last = _mixer_b(l, x2, state_conv_b, state_rglru, w_in_b, conv_b_w, conv_b_b,
                                         w_r_b, b_r, w_i_b, b_i, lru_lambda, prompt, seqs)
            x2 = _merge(l, x2, ya, yb, w_in_b, w_a_out_b, w_b_out_b, w_o_b, ln1_g, ln1_b, bm)
            x2, new_f = _ffn(l, x2, state_conv_ffn, w_up_b, ffn_conv_w, ffn_conv_b, w_down_b, ln2_g, ln2_b, prompt, seqs)
            x2 = _ple(l, x2, grp["p"], w_pg_b, b_pg, w_pe_b, ln3_g, ln3_b, bm)
            grp["x"] = x2
            states[gi].append((new_a, new_b, h_last, new_f))

    stack = lambda gi, k: jnp.stack([s[k] for s in states[gi]])
    return (groups[0]["x"].reshape(bp, tp, D_MODEL), groups[1]["x"].reshape(bs, ts, D_MODEL),
            stack(0, 0), stack(0, 1), stack(0, 2), stack(0, 3),
            stack(1, 0), stack(1, 1), stack(1, 2), stack(1, 3))
```

```python
import functools

import jax
import jax.numpy as jnp
from jax import lax
from jax.experimental import pallas as pl
from jax.experimental.pallas import tpu as pltpu

F32 = jnp.float32
BF16 = jnp.bfloat16

D_MODEL = 2048
DEPTH = 2
D_CONV = D_MODEL // 2
CONV_A_WIDTH = 31
D_LRU = D_MODEL
LRU_HEADS = 16
LRU_HEAD_DIM = D_LRU // LRU_HEADS
CONV_B_WIDTH = 4
LRU_C = 8.0
D_FF = 3 * D_MODEL
FFN_CONV_WIDTH = 3
D_PLE = 256
LN_EPS = 1e-5
ALPHA = (2.0 * DEPTH) ** 0.25
OFF_AV, OFF_AG, OFF_BX, OFF_BG, OFF_GA, OFF_GB = 0, D_CONV, 2 * D_CONV, 2 * D_CONV + D_LRU, 2 * D_CONV + 2 * D_LRU, 2 * D_CONV + 2 * D_LRU + D_MODEL

SUBLANES = 8
VMEM_LIMIT_BYTES = 56 * 1024 * 1024

BM_PROMPT = 512
BM_SAMPLE = 512
BM_MIXER_A_SAMPLE = 256
BM_MIXER_B = 256
CHUNK = 512
CM = 512
CF = 1024
NSPLIT = 2
RB = 64
LN_RB = 32
LN_UNROLL = 4
LC = 256
HIST_A = 32
HIST = SUBLANES


def _dot(a, b):
    return jnp.dot(a, b, preferred_element_type=F32)


def _ln(v, g, b):
    mu = jnp.mean(v, axis=-1, keepdims=True)
    c = v - mu
    var = jnp.mean(c * c, axis=-1, keepdims=True)
    return c * lax.rsqrt(var + LN_EPS) * g + b


def _residual_ln_rows(x_ref, acc_ref, g_ref, b_ref, o_ref, bm):
    def body(r, carry):
        rows = pl.ds(pl.multiple_of(r * LN_RB, LN_RB), LN_RB)
        v = ALPHA * x_ref[rows, :] + acc_ref[rows, :]
        o_ref[rows, :] = _ln(v, g_ref[...], b_ref[...])
        return carry
    lax.fori_loop(0, bm // LN_RB, body, 0, unroll=LN_UNROLL)


def _scan8(a, u):
    n, cols = a.shape
    a = a.reshape(n // SUBLANES, SUBLANES, cols)
    u = u.reshape(n // SUBLANES, SUBLANES, cols)
    rows = lax.broadcasted_iota(jnp.int32, a.shape, 1)
    for d in (1, 2, 4):
        m = rows >= d
        a_sh = pltpu.roll(a, d, 1)
        u_sh = pltpu.roll(u, d, 1)
        u = jnp.where(m, a * u_sh + u, u)
        a = jnp.where(m, a * a_sh, a)
    return a.reshape(n, cols), u.reshape(n, cols)


def _gate_dots(cb, wr_ref, wi_ref, c):
    hpc = CHUNK // LRU_HEAD_DIM
    rs, gs = [], []
    for hh in range(hpc):
        cbh = cb[:, hh * LRU_HEAD_DIM:(hh + 1) * LRU_HEAD_DIM].astype(BF16)
        rs.append(_dot(cbh, wr_ref[c * hpc + hh]))
        gs.append(_dot(cbh, wi_ref[c * hpc + hh]))
    return jnp.concatenate(rs, axis=-1), jnp.concatenate(gs, axis=-1)


def _lru_coeffs(cb, r_pre, g_pre, br_ref, bi_ref, lam_ref, c, first_tile):
    cols = slice(c * CHUNK, (c + 1) * CHUNK)
    r = jax.nn.sigmoid(r_pre + br_ref[:, cols])
    gi = jax.nn.sigmoid(g_pre + bi_ref[:, cols])
    log_a = (-LRU_C) * r * jax.nn.softplus(-lam_ref[:, cols])
    a = jnp.exp(log_a)
    th = jnp.tanh(log_a)
    mult = jnp.sqrt(-2.0 * th / (1.0 - th))
    if first_tile is not None:
        row0 = lax.broadcasted_iota(jnp.int32, mult.shape, 0) == 0
        mult = jnp.where(jnp.logical_and(row0, first_tile), 1.0, mult)
    return a, mult * gi * cb


def _conv31_taps():
    taps = []
    for r in range(SUBLANES):
        for q in range(HIST_A // SUBLANES):
            j = SUBLANES * q + r
            if j < CONV_A_WIDTH:
                taps.append((r, q, CONV_A_WIDTH - 1 - j))
    return taps


def _conv31_prompt(u_scr, cw_ref, cb_ref, ca_scr, bm):
    def body(blk, carry):
        base = pl.multiple_of(blk * RB, RB)
        for lc in range(D_CONV // LC):
            lanes = pl.ds(lc * LC, LC)
            win = u_scr[pl.ds(base, RB + HIST_A), lanes]
            acc = jnp.broadcast_to(cb_ref[:, lanes], (RB, LC))
            v, v_r = win, 0
            for r, q, k in _conv31_taps():
                if r != v_r:
                    v, v_r = pltpu.roll(win, r, 0), r
                off = HIST_A - SUBLANES * q
                acc = acc + cw_ref[k:k + 1, lanes] * v[off:off + RB, :]
            ca_scr[pl.ds(base, RB), lanes] = acc
        return carry
    lax.fori_loop(0, bm // RB, body, 0)


def _conv31_sample(ext_scr, cw_ref, cb_ref, ca_scr, bm):
    sb_n = RB // SUBLANES
    def body(blk, carry):
        s0 = pl.multiple_of(blk * sb_n, sb_n)
        for lc in range(D_CONV // LC):
            lanes = pl.ds(lc * LC, LC)
            win = ext_scr[pl.ds(s0, sb_n), :, lanes]
            acc = jnp.broadcast_to(cb_ref[:, lanes][None], (sb_n, SUBLANES, LC))
            v, v_r = win, 0
            for r, q, k in _conv31_taps():
                if r != v_r:
                    v, v_r = pltpu.roll(win, r, 1), r
                off = HIST_A - SUBLANES * q
                acc = acc + cw_ref[k:k + 1, lanes][None] * v[:, off:off + SUBLANES, :]
            ca_scr[pl.ds(pl.multiple_of(blk * RB, RB), RB), lanes] = acc.reshape(RB, LC)
        return carry
    lax.fori_loop(0, bm // RB, body, 0)


def _ln_silu_rows(ca_scr, lg_ref, lb_ref, ya_ref, bm):
    def body(r, carry):
        rows = pl.ds(pl.multiple_of(r * RB, RB), RB)
        y = _ln(ca_scr[rows, :], lg_ref[...], lb_ref[...])
        ya_ref[rows, :] = (y * jax.nn.sigmoid(y)).astype(BF16)
        return carry
    lax.fori_loop(0, bm // RB, body, 0, unroll=2)


def _a1_prompt_kernel(x_ref, wv_ref, wg_ref, cw_ref, cb_ref, lg_ref, lb_ref, ya_ref, na_ref, u_scr, ca_scr, *, bm, tps):
    t = pl.program_id(0) % tps

    @pl.when(t == 0)
    def _():
        u_scr[0:HIST_A, :] = jnp.zeros((HIST_A, D_CONV), F32)

    @pl.when(t != 0)
    def _():
        u_scr[0:HIST_A, :] = u_scr[bm:bm + HIST_A, :]

    xb = x_ref[...].astype(BF16)
    for c in range(D_CONV // CHUNK):
        cols = slice(c * CHUNK, (c + 1) * CHUNK)
        av = _dot(xb, wv_ref[:, cols])
        ag = _dot(xb, wg_ref[:, cols])
        u_scr[HIST_A:HIST_A + bm, cols] = av * jax.nn.sigmoid(ag)
    na_ref[...] = u_scr[HIST_A + bm - (CONV_A_WIDTH - 1):HIST_A + bm, :]
    _conv31_prompt(u_scr, cw_ref, cb_ref, ca_scr, bm)
    _ln_silu_rows(ca_scr, lg_ref, lb_ref, ya_ref, bm)


def _a1_sample_kernel(x_ref, st_ref, wv_ref, wg_ref, cw_ref, cb_ref, lg_ref, lb_ref, ya_ref, na_ref, ext_scr, ca_scr, *, bm):
    s_n = bm // SUBLANES
    hist = CONV_A_WIDTH - 1
    xb = x_ref[...].astype(BF16)
    ext_scr[:, 0:SUBLANES, :] = jnp.zeros((s_n, SUBLANES, D_CONV), F32)
    ext_scr[:, HIST_A - hist:HIST_A, :] = st_ref[...]
    for c in range(D_CONV // CHUNK):
        cols = slice(c * CHUNK, (c + 1) * CHUNK)
        av = _dot(xb, wv_ref[:, cols])
        ag = _dot(xb, wg_ref[:, cols])
        ext_scr[:, HIST_A:HIST_A + SUBLANES, cols] = (av * jax.nn.sigmoid(ag)).reshape(s_n, SUBLANES, CHUNK)
    na_ref[...] = ext_scr[:, HIST_A + SUBLANES - hist:HIST_A + SUBLANES, :]
    _conv31_sample(ext_scr, cw_ref, cb_ref, ca_scr, bm)
    _ln_silu_rows(ca_scr, lg_ref, lb_ref, ya_ref, bm)


def _a2_prompt_kernel(x_ref, wbx_ref, wbg_ref, cw_ref, cb_ref, wr_ref, br_ref, wi_ref, bi_ref, lam_ref,
                      yb_ref, nb_ref, hl_ref, bx_scr, a_scr, u_scr, g_scr, hc_scr, *, bm, tps):
    t = pl.program_id(0) % tps
    hist = CONV_B_WIDTH - 1
    nch = D_LRU // CHUNK

    @pl.when(t == 0)
    def _():
        bx_scr[0:HIST, :] = jnp.zeros((HIST, D_LRU), F32)
        hc_scr[...] = jnp.zeros((SUBLANES, D_LRU), F32)

    @pl.when(t != 0)
    def _():
        bx_scr[0:HIST, :] = bx_scr[bm:bm + HIST, :]

    xb = x_ref[...].astype(BF16)
    first_tile = t == 0
    chunk = lambda c: slice(c * CHUNK, (c + 1) * CHUNK)
    bx_next = _dot(xb, wbx_ref[:, chunk(0)])
    for c in range(nch):
        cols = chunk(c)
        bx = bx_next
        bx_scr[HIST:HIST + bm, cols] = bx
        cb = cb_ref[:, cols] + cw_ref[hist:hist + 1, cols] * bx
        for k in range(hist):
            cb = cb + cw_ref[k:k + 1, cols] * bx_scr[HIST - hist + k:HIST - hist + k + bm, cols]
        r_pre, g_pre = _gate_dots(cb, wr_ref, wi_ref, c)
        if c + 1 < nch:
            bx_next = _dot(xb, wbx_ref[:, chunk(c + 1)])
        bg = _dot(xb, wbg_ref[:, cols])
        a, u = _lru_coeffs(cb, r_pre, g_pre, br_ref, bi_ref, lam_ref, c, first_tile)
        a, u = _scan8(a, u)
        a_scr[:, cols] = a
        u_scr[:, cols] = u
        g_scr[:, cols] = jax.nn.gelu(bg)
    nb_ref[...] = bx_scr[HIST + bm - hist:HIST + bm, :]

    def body(r, carry):
        rows = pl.ds(pl.multiple_of(r * 2 * SUBLANES, 2 * SUBLANES), 2 * SUBLANES)
        a2, u2 = a_scr[rows, :], u_scr[rows, :]
        h0 = u2[0:SUBLANES, :] + a2[0:SUBLANES, :] * carry
        mid = jnp.broadcast_to(h0[SUBLANES - 1:SUBLANES, :], (SUBLANES, D_LRU))
        h1 = u2[SUBLANES:, :] + a2[SUBLANES:, :] * mid
        yb_ref[rows, :] = (jnp.concatenate([h0, h1], axis=0) * g_scr[rows, :]).astype(BF16)
        return jnp.broadcast_to(h1[SUBLANES - 1:SUBLANES, :], (SUBLANES, D_LRU))
    carry = lax.fori_loop(0, bm // (2 * SUBLANES), body, hc_scr[...])
    hc_scr[...] = carry
    hl_ref[...] = carry[0:1, :]


def _a2_sample_kernel(x_ref, stb_ref, h0_ref, wbx_ref, wbg_ref, cw_ref, cb_ref, wr_ref, br_ref, wi_ref, bi_ref, lam_ref,
                      yb_ref, nb_ref, hl_ref, ext_scr, *, bm):
    s_n = bm // SUBLANES
    hist = CONV_B_WIDTH - 1
    xb = x_ref[...].astype(BF16)
    ext_scr[:, HIST - hist:HIST, :] = stb_ref[...]
    nch = D_LRU // CHUNK
    chunk = lambda c: slice(c * CHUNK, (c + 1) * CHUNK)
    bx_next = _dot(xb, wbx_ref[:, chunk(0)])
    for c in range(nch):
        cols = chunk(c)
        bx3 = bx_next.reshape(s_n, SUBLANES, CHUNK)
        ext_scr[:, HIST:HIST + SUBLANES, cols] = bx3
        cb3 = cb_ref[:, cols][None] + cw_ref[hist:hist + 1, cols][None] * bx3
        for k in range(hist):
            cb3 = cb3 + cw_ref[k:k + 1, cols][None] * ext_scr[:, HIST - hist + k:HIST - hist + k + SUBLANES, cols]
        cb = cb3.reshape(bm, CHUNK)
        r_pre, g_pre = _gate_dots(cb, wr_ref, wi_ref, c)
        if c + 1 < nch:
            bx_next = _dot(xb, wbx_ref[:, chunk(c + 1)])
        bg = _dot(xb, wbg_ref[:, cols])
        a, u = _lru_coeffs(cb, r_pre, g_pre, br_ref, bi_ref, lam_ref, c, None)
        a, u = _scan8(a, u)
        h3 = u.reshape(s_n, SUBLANES, CHUNK) + a.reshape(s_n, SUBLANES, CHUNK) * h0_ref[:, cols][:, None, :]
        hl_ref[:, cols] = h3[:, SUBLANES - 1, :]
        yb_ref[:, cols] = (h3.reshape(bm, CHUNK) * jax.nn.gelu(bg)).astype(BF16)
    nb_ref[...] = ext_scr[:, HIST + SUBLANES - hist:HIST + SUBLANES, :]


def _merge_kernel(x_ref, ya_ref, yb_ref, wga_ref, wgb_ref, wa_ref, wb_ref, wo_ref, g_ref, b_ref, o_ref, xb_scr, acc_scr, *, bm, nc):
    c = pl.program_id(1)

    @pl.when(c == 0)
    def _():
        xb_scr[...] = x_ref[...].astype(BF16)
        acc_scr[...] = jnp.zeros((bm, D_MODEL), F32)

    sb = bm // NSPLIT

    def up(s):
        rows = slice(s * sb, (s + 1) * sb)
        xb = xb_scr[rows, :]
        return (_dot(xb, wga_ref[...]), _dot(ya_ref[rows, :], wa_ref[...]),
                _dot(xb, wgb_ref[...]), _dot(yb_ref[rows, :], wb_ref[...]))

    def down(s, ga, out_a, gb, out_b):
        merged = jax.nn.sigmoid(ga) * out_a + jax.nn.sigmoid(gb) * out_b
        acc_scr[s * sb:(s + 1) * sb, :] += _dot(merged.astype(BF16), wo_ref[...])

    pending = up(0)
    for s in range(NSPLIT):
        nxt = up(s + 1) if s + 1 < NSPLIT else None
        down(s, *pending)
        pending = nxt

    @pl.when(c == nc - 1)
    def _():
        _residual_ln_rows(x_ref, acc_scr, g_ref, b_ref, o_ref, bm)


def _ffn_prompt_kernel(x_ref, wfu_ref, wfg_ref, cw_ref, cb_ref, wd_ref, g_ref, b_ref, o_ref, nf_ref,
                       xb_scr, acc_scr, fg_scr, carry_scr, *, bm, tps, nc):
    t = pl.program_id(0) % tps
    c = pl.program_id(1)
    hist = FFN_CONV_WIDTH - 1

    @pl.when(c == 0)
    def _():
        xb_scr[...] = x_ref[...].astype(BF16)
        acc_scr[...] = jnp.zeros((bm, D_MODEL), F32)

    @pl.when(t == 0)
    def _():
        fg_scr[0:HIST, :] = jnp.zeros((HIST, CF), F32)

    @pl.when(t != 0)
    def _():
        fg_scr[0:HIST, :] = carry_scr[c]

    sb = bm // NSPLIT

    def up(s):
        xb = xb_scr[s * sb:(s + 1) * sb, :]
        fg = _dot(xb, wfg_ref[...])
        fg_scr[HIST + s * sb:HIST + (s + 1) * sb, :] = fg
        return _dot(xb, wfu_ref[...]), fg

    def down(s, fu, fg):
        r0 = s * sb
        fgc = cb_ref[...] + cw_ref[hist:hist + 1, :] * fg
        for k in range(hist):
            fgc = fgc + cw_ref[k:k + 1, :] * fg_scr[HIST - hist + k + r0:HIST - hist + k + r0 + sb, :]
        hf = jax.nn.gelu(fgc) * fu
        acc_scr[r0:r0 + sb, :] += _dot(hf.astype(BF16), wd_ref[...])

    pending = up(0)
    for s in range(NSPLIT):
        nxt = up(s + 1) if s + 1 < NSPLIT else None
        down(s, *pending)
        pending = nxt
    carry_scr[c] = fg_scr[bm:bm + HIST, :]
    nf_ref[...] = fg_scr[HIST + bm - hist:HIST + bm, :]

    @pl.when(c == nc - 1)
    def _():
        _residual_ln_rows(x_ref, acc_scr, g_ref, b_ref, o_ref, bm)


def _ffn_sample_kernel(x_ref, st_ref, wfu_ref, wfg_ref, cw_ref, cb_ref, wd_ref, g_ref, b_ref, o_ref, nf_ref,
                       xb_scr, acc_scr, ext_scr, *, bm, nc):
    c = pl.program_id(1)
    s_n = bm // SUBLANES
    hist = FFN_CONV_WIDTH - 1

    @pl.when(c == 0)
    def _():
        xb_scr[...] = x_ref[...].astype(BF16)
        acc_scr[...] = jnp.zeros((bm, D_MODEL), F32)

    ext_scr[:, HIST - hist:HIST, :] = st_ref[...]
    sb = bm // NSPLIT
    sq = sb // SUBLANES

    def up(s):
        xb = xb_scr[s * sb:(s + 1) * sb, :]
        fg3 = _dot(xb, wfg_ref[...]).reshape(sq, SUBLANES, CF)
        ext_scr[s * sq:(s + 1) * sq, HIST:HIST + SUBLANES, :] = fg3
        return _dot(xb, wfu_ref[...]), fg3

    def down(s, fu, fg3):
        fgc3 = cb_ref[...][None] + cw_ref[hist:hist + 1, :][None] * fg3
        for k in range(hist):
            fgc3 = fgc3 + cw_ref[k:k + 1, :][None] * ext_scr[s * sq:(s + 1) * sq, HIST - hist + k:HIST - hist + k + SUBLANES, :]
        hf = jax.nn.gelu(fgc3.reshape(sb, CF)) * fu
        acc_scr[s * sb:(s + 1) * sb, :] += _dot(hf.astype(BF16), wd_ref[...])

    pending = up(0)
    for s in range(NSPLIT):
        nxt = up(s + 1) if s + 1 < NSPLIT else None
        down(s, *pending)
        pending = nxt
    nf_ref[...] = ext_scr[:, HIST + SUBLANES - hist:HIST + SUBLANES, :]

    @pl.when(c == nc - 1)
    def _():
        _residual_ln_rows(x_ref, acc_scr, g_ref, b_ref, o_ref, bm)


def _ple_kernel(x_ref, p_ref, wpg_ref, bpg_ref, wpe_ref, g_ref, b_ref, o_ref, e_scr, *, bm):
    xb = x_ref[...].astype(BF16)
    pb = p_ref[...].astype(BF16)
    for c in range(D_MODEL // CHUNK):
        cols = slice(c * CHUNK, (c + 1) * CHUNK)
        gate = jax.nn.sigmoid(_dot(xb, wpg_ref[:, cols]) + bpg_ref[:, cols])
        e_scr[:, cols] = gate * _dot(pb, wpe_ref[:, cols])
    _residual_ln_rows(x_ref, e_scr, g_ref, b_ref, o_ref, bm)


def _params(n_axes):
    return pltpu.CompilerParams(dimension_semantics=("arbitrary",) * n_axes, vmem_limit_bytes=VMEM_LIMIT_BYTES)


def _resident(shape, index_map):
    return pl.BlockSpec(shape, index_map, pipeline_mode=pl.Buffered(1))


def _row(l, width):
    return _resident((None, 1, width), lambda *idx: (l, 0, 0))


def _mixer_a(l, x2, state, w_in_b, conv_w, conv_b, ln_g, ln_b, prompt, seqs):
    n = x2.shape[0]
    bm = BM_PROMPT if prompt else BM_MIXER_A_SAMPLE
    hist = CONV_A_WIDTH - 1
    weights = [
        _resident((None, D_MODEL, D_CONV), lambda i: (l, 0, OFF_AV // D_CONV)),
        _resident((None, D_MODEL, D_CONV), lambda i: (l, 0, OFF_AG // D_CONV)),
        _resident((None, CONV_A_WIDTH, D_CONV), lambda i: (l, 0, 0)),
        _row(l, D_CONV), _row(l, D_CONV), _row(l, D_CONV),
    ]
    x_spec = pl.BlockSpec((bm, D_MODEL), lambda i: (i, 0))
    ya_spec = pl.BlockSpec((bm, D_CONV), lambda i: (i, 0))
    out_shape = [jax.ShapeDtypeStruct((n, D_CONV), BF16), jax.ShapeDtypeStruct((seqs, hist, D_CONV), F32)]
    args = (w_in_b, w_in_b, conv_w, conv_b, ln_g, ln_b)
    if prompt:
        tps = (n // seqs) // bm
        return pl.pallas_call(
            functools.partial(_a1_prompt_kernel, bm=bm, tps=tps),
            grid=(n // bm,),
            in_specs=[x_spec] + weights,
            out_specs=[ya_spec, pl.BlockSpec((None, hist, D_CONV), lambda i: (i // tps, 0, 0))],
            out_shape=out_shape,
            scratch_shapes=[pltpu.VMEM((HIST_A + bm, D_CONV), F32), pltpu.VMEM((bm, D_CONV), F32)],
            compiler_params=_params(1), name=f"mixer_a_prompt_l{l}",
        )(x2, *args)
    s_n = bm // SUBLANES
    return pl.pallas_call(
        functools.partial(_a1_sample_kernel, bm=bm),
        grid=(n // bm,),
        in_specs=[x_spec, pl.BlockSpec((None, s_n, hist, D_CONV), lambda i: (l, i, 0, 0))] + weights,
        out_specs=[ya_spec, pl.BlockSpec((s_n, hist, D_CONV), lambda i: (i, 0, 0))],
        out_shape=out_shape,
        scratch_shapes=[pltpu.VMEM((s_n, HIST_A + SUBLANES, D_CONV), F32), pltpu.VMEM((bm, D_CONV), F32)],
        compiler_params=_params(1), name=f"mixer_a_sample_l{l}",
    )(x2, state, *args)


def _mixer_b(l, x2, state_b, state_h, w_in_b, conv_w, conv_b, w_r_b, b_r, w_i_b, b_i, lam, prompt, seqs):
    n = x2.shape[0]
    bm = BM_MIXER_B
    hist = CONV_B_WIDTH - 1
    gate_w = _resident((None, LRU_HEADS, LRU_HEAD_DIM, LRU_HEAD_DIM), lambda i: (l, 0, 0, 0))
    weights = [
        _resident((None, D_MODEL, D_LRU), lambda i: (l, 0, OFF_BX // D_LRU)),
        _resident((None, D_MODEL, D_LRU), lambda i: (l, 0, OFF_BG // D_LRU)),
        _resident((None, CONV_B_WIDTH, D_LRU), lambda i: (l, 0, 0)),
        _row(l, D_LRU), gate_w, _row(l, D_LRU), gate_w, _row(l, D_LRU), _row(l, D_LRU),
    ]
    x_spec = pl.BlockSpec((bm, D_MODEL), lambda i: (i, 0))
    yb_spec = pl.BlockSpec((bm, D_LRU), lambda i: (i, 0))
    args = (w_in_b, w_in_b, conv_w, conv_b, w_r_b, b_r, w_i_b, b_i, lam)
    if prompt:
        tps = (n // seqs) // bm
        yb, nb, hl = pl.pallas_call(
            functools.partial(_a2_prompt_kernel, bm=bm, tps=tps),
            grid=(n // bm,),
            in_specs=[x_spec] + weights,
            out_specs=[yb_spec,
                       pl.BlockSpec((None, hist, D_LRU), lambda i: (i // tps, 0, 0)),
                       pl.BlockSpec((None, 1, D_LRU), lambda i: (i // tps, 0, 0))],
            out_shape=[jax.ShapeDtypeStruct((n, D_LRU), BF16),
                       jax.ShapeDtypeStruct((seqs, hist, D_LRU), F32),
                       jax.ShapeDtypeStruct((seqs, 1, D_LRU), F32)],
            scratch_shapes=[pltpu.VMEM((HIST + bm, D_LRU), F32), pltpu.VMEM((bm, D_LRU), F32),
                            pltpu.VMEM((bm, D_LRU), F32), pltpu.VMEM((bm, D_LRU), F32),
                            pltpu.VMEM((SUBLANES, D_LRU), F32)],
            compiler_params=_params(1), name=f"mixer_b_prompt_l{l}",
        )(x2, *args)
        return yb, nb, hl.reshape(seqs, D_LRU)
    s_n = bm // SUBLANES
    return pl.pallas_call(
        functools.partial(_a2_sample_kernel, bm=bm),
        grid=(n // bm,),
        in_specs=[x_spec,
                  pl.BlockSpec((None, s_n, hist, D_LRU), lambda i: (l, i, 0, 0)),
                  pl.BlockSpec((None, s_n, D_LRU), lambda i: (l, i, 0))] + weights,
        out_specs=[yb_spec,
                   pl.BlockSpec((s_n, hist, D_LRU), lambda i: (i, 0, 0)),
                   pl.BlockSpec((s_n, D_LRU), lambda i: (i, 0))],
        out_shape=[jax.ShapeDtypeStruct((n, D_LRU), BF16),
                   jax.ShapeDtypeStruct((seqs, hist, D_LRU), F32),
                   jax.ShapeDtypeStruct((seqs, D_LRU), F32)],
        scratch_shapes=[pltpu.VMEM((s_n, HIST + SUBLANES, D_LRU), F32)],
        compiler_params=_params(1), name=f"mixer_b_sample_l{l}",
    )(x2, state_b, state_h, *args)


def _merge(l, x2, ya, yb, w_in_b, w_a_out_b, w_b_out_b, w_o_b, g, b, prompt):
    n = x2.shape[0]
    bm = BM_PROMPT if prompt else BM_SAMPLE
    nc = D_MODEL // CM
    return pl.pallas_call(
        functools.partial(_merge_kernel, bm=bm, nc=nc),
        grid=(n // bm, nc),
        in_specs=[
            pl.BlockSpec((bm, D_MODEL), lambda i, c: (i, 0)),
            pl.BlockSpec((bm, D_CONV), lambda i, c: (i, 0)),
            pl.BlockSpec((bm, D_LRU), lambda i, c: (i, 0)),
            pl.BlockSpec((None, D_MODEL, CM), lambda i, c: (l, 0, OFF_GA // CM + c)),
            pl.BlockSpec((None, D_MODEL, CM), lambda i, c: (l, 0, OFF_GB // CM + c)),
            pl.BlockSpec((None, D_CONV, CM), lambda i, c: (l, 0, c)),
            pl.BlockSpec((None, D_LRU, CM), lambda i, c: (l, 0, c)),
            pl.BlockSpec((None, CM, D_MODEL), lambda i, c: (l, c, 0)),
            _row(l, D_MODEL), _row(l, D_MODEL),
        ],
        out_specs=pl.BlockSpec((bm, D_MODEL), lambda i, c: (i, 0)),
        out_shape=jax.ShapeDtypeStruct((n, D_MODEL), F32),
        scratch_shapes=[pltpu.VMEM((bm, D_MODEL), BF16), pltpu.VMEM((bm, D_MODEL), F32)],
        compiler_params=_params(2), name=f"merge_{'prompt' if prompt else 'sample'}_l{l}",
    )(x2, ya, yb, w_in_b, w_in_b, w_a_out_b, w_b_out_b, w_o_b, g, b)


def _ffn(l, x2, state, w_up_b, conv_w, conv_b, w_down_b, g, b, prompt, seqs):
    n = x2.shape[0]
    bm = BM_PROMPT if prompt else BM_SAMPLE
    nc = D_FF // CF
    hist = FFN_CONV_WIDTH - 1
    weights = [
        pl.BlockSpec((None, D_MODEL, CF), lambda i, c: (l, 0, c)),
        pl.BlockSpec((None, D_MODEL, CF), lambda i, c: (l, 0, nc + c)),
        pl.BlockSpec((None, FFN_CONV_WIDTH, CF), lambda i, c: (l, 0, c)),
        pl.BlockSpec((None, 1, CF), lambda i, c: (l, 0, c)),
        pl.BlockSpec((None, CF, D_MODEL), lambda i, c: (l, c, 0)),
        _row(l, D_MODEL), _row(l, D_MODEL),
    ]
    x_spec = pl.BlockSpec((bm, D_MODEL), lambda i, c: (i, 0))
    o_spec = pl.BlockSpec((bm, D_MODEL), lambda i, c: (i, 0))
    x_shape = jax.ShapeDtypeStruct((n, D_MODEL), F32)
    scratch = [pltpu.VMEM((bm, D_MODEL), BF16), pltpu.VMEM((bm, D_MODEL), F32)]
    args = (w_up_b, w_up_b, conv_w, conv_b, w_down_b, g, b)
    if prompt:
        tps = (n // seqs) // bm
        x_out, nf_tiles = pl.pallas_call(
            functools.partial(_ffn_prompt_kernel, bm=bm, tps=tps, nc=nc),
            grid=(n // bm, nc),
            in_specs=[x_spec] + weights,
            out_specs=[o_spec, pl.BlockSpec((None, hist, CF), lambda i, c: (i, 0, c))],
            out_shape=[x_shape, jax.ShapeDtypeStruct((n // bm, hist, D_FF), F32)],
            scratch_shapes=scratch + [pltpu.VMEM((HIST + bm, CF), F32), pltpu.VMEM((nc, HIST, CF), F32)],
            compiler_params=_params(2), name=f"ffn_prompt_l{l}",
        )(x2, *args)
        return x_out, nf_tiles.reshape(seqs, tps, hist, D_FF)[:, tps - 1]
    s_n = bm // SUBLANES
    return pl.pallas_call(
        functools.partial(_ffn_sample_kernel, bm=bm, nc=nc),
        grid=(n // bm, nc),
        in_specs=[x_spec, pl.BlockSpec((None, s_n, hist, CF), lambda i, c: (l, i, 0, c))] + weights,
        out_specs=[o_spec, pl.BlockSpec((s_n, hist, CF), lambda i, c: (i, 0, c))],
        out_shape=[x_shape, jax.ShapeDtypeStruct((seqs, hist, D_FF), F32)],
        scratch_shapes=scratch + [pltpu.VMEM((s_n, HIST + SUBLANES, CF), F32)],
        compiler_params=_params(2), name=f"ffn_sample_l{l}",
    )(x2, state, *args)


def _ple(l, x2, p2, w_pg_b, b_pg, w_pe_b, g, b, prompt):
    n = x2.shape[0]
    bm = BM_PROMPT if prompt else BM_SAMPLE
    return pl.pallas_call(
        functools.partial(_ple_kernel, bm=bm),
        grid=(n // bm,),
        in_specs=[
            pl.BlockSpec((bm, D_MODEL), lambda i: (i, 0)),
            pl.BlockSpec((None, bm, D_PLE), lambda i: (l, i, 0)),
            _resident((None, D_MODEL, D_MODEL), lambda i: (l, 0, 0)),
            _row(l, D_MODEL),
            _resident((None, D_PLE, D_MODEL), lambda i: (l, 0, 0)),
            _row(l, D_MODEL), _row(l, D_MODEL),
        ],
        out_specs=pl.BlockSpec((bm, D_MODEL), lambda i: (i, 0)),
        out_shape=jax.ShapeDtypeStruct((n, D_MODEL), F32),
        scratch_shapes=[pltpu.VMEM((bm, D_MODEL), F32)],
        compiler_params=_params(1), name=f"ple_{'prompt' if prompt else 'sample'}_l{l}",
    )(x2, p2, w_pg_b, b_pg, w_pe_b, g, b)


def kernel(x_prompt, x_sample, state_conv_a, state_conv_b, state_rglru, state_conv_ffn, p_prompt, p_sample, w_in, conv_a_w, conv_a_b, ln_a_g, ln_a_b, w_a_out, conv_b_w, conv_b_b, w_r, b_r, w_i, b_i, lru_lambda, w_b_out, w_o, ln1_g, ln1_b, w_up, ffn_conv_w, ffn_conv_b, w_down, ln2_g, ln2_b, w_pe, w_pg, b_pg, ln3_g, ln3_b):
    bp, tp, _ = x_prompt.shape
    bs, ts, _ = x_sample.shape
    assert ts == SUBLANES and tp % BM_PROMPT == 0 and tp % BM_MIXER_B == 0
    assert all((bs * ts) % b == 0 for b in (BM_SAMPLE, BM_MIXER_A_SAMPLE, BM_MIXER_B))
    assert w_in.shape == (DEPTH, D_MODEL, OFF_GB + D_MODEL)

    w_in_b, w_a_out_b, w_b_out_b, w_o_b = (w.astype(BF16) for w in (w_in, w_a_out, w_b_out, w_o))
    w_up_b, w_down_b, w_pe_b, w_pg_b = (w.astype(BF16) for w in (w_up, w_down, w_pe, w_pg))
    w_r_b, w_i_b = w_r.astype(BF16), w_i.astype(BF16)
    row = lambda v: v.reshape(DEPTH, 1, v.shape[-1])
    conv_a_b, ln_a_g, ln_a_b, conv_b_b, b_r, b_i, lru_lambda = map(row, (conv_a_b, ln_a_g, ln_a_b, conv_b_b, b_r, b_i, lru_lambda))
    ln1_g, ln1_b, ffn_conv_b, ln2_g, ln2_b, b_pg, ln3_g, ln3_b = map(row, (ln1_g, ln1_b, ffn_conv_b, ln2_g, ln2_b, b_pg, ln3_g, ln3_b))

    groups = [
        dict(x=x_prompt.reshape(bp * tp, D_MODEL), p=p_prompt.reshape(DEPTH, bp * tp, D_PLE), prompt=True, seqs=bp),
        dict(x=x_sample.reshape(bs * ts, D_MODEL), p=p_sample.reshape(DEPTH, bs * ts, D_PLE), prompt=False, seqs=bs),
    ]
    states = [[], []]
    for l in range(DEPTH):
        for gi, grp in enumerate(groups):
            x2, prompt, seqs = grp["x"], grp["prompt"], grp["seqs"]
            ya, new_a = _mixer_a(l, x2, state_conv_a, w_in_b, conv_a_w, conv_a_b, ln_a_g, ln_a_b, prompt, seqs)
            yb, new_b, h_last = _mixer_b(l, x2, state_conv_b, state_rglru, w_in_b, conv_b_w, conv_b_b,
                                         w_r_b, b_r, w_i_b, b_i, lru_lambda, prompt, seqs)
            x2 = _merge(l, x2, ya, yb, w_in_b, w_a_out_b, w_b_out_b, w_o_b, ln1_g, ln1_b, prompt)
            x2, new_f = _ffn(l, x2, state_conv_ffn, w_up_b, ffn_conv_w, ffn_conv_b, w_down_b, ln2_g, ln2_b, prompt, seqs)
            x2 = _ple(l, x2, grp["p"], w_pg_b, b_pg, w_pe_b, ln3_g, ln3_b, prompt)
            grp["x"] = x2
            states[gi].append((new_a, new_b, h_last, new_f))

    stack = lambda gi, k: jnp.stack([s[k] for s in states[gi]])
    return (groups[0]["x"].reshape(bp, tp, D_MODEL), groups[1]["x"].reshape(bs, ts, D_MODEL),
            stack(0, 0), stack(0, 1), stack(0, 2), stack(0, 3),
            stack(1, 0), stack(1, 1), stack(1, 2), stack(1, 3))
```

```python
import functools

import jax
import jax.numpy as jnp
from jax import lax
from jax.experimental import pallas as pl
from jax.experimental.pallas import tpu as pltpu

F32 = jnp.float32
BF16 = jnp.bfloat16

D_MODEL = 2048
DEPTH = 2
D_CONV = D_MODEL // 2
CONV_A_WIDTH = 31
D_LRU = D_MODEL
LRU_HEADS = 16
LRU_HEAD_DIM = D_LRU // LRU_HEADS
CONV_B_WIDTH = 4
LRU_C = 8.0
D_FF = 3 * D_MODEL
FFN_CONV_WIDTH = 3
D_PLE = 256
LN_EPS = 1e-5
ALPHA = (2.0 * DEPTH) ** 0.25
OFF_AV, OFF_AG, OFF_BX, OFF_BG, OFF_GA, OFF_GB = 0, D_CONV, 2 * D_CONV, 2 * D_CONV + D_LRU, 2 * D_CONV + 2 * D_LRU, 2 * D_CONV + 2 * D_LRU + D_MODEL

SUBLANES = 8
VMEM_LIMIT_BYTES = 56 * 1024 * 1024

BM_PROMPT = 512
BM_SAMPLE = 512
BM_MIXER_B = 256
CHUNK = 512
CM = 512
CF = 1024
NSPLIT = 2
TS = 8
S_MIX = 32
CF_S = 256
NSPLIT_S = 4
RB = 64
LN_RB = 32
LN_UNROLL = 4
LC = 256
HIST_A = 32
HIST = SUBLANES


def _dot(a, b):
    return jnp.dot(a, b, preferred_element_type=F32)


def _sigmoid(x):
    return 0.5 * jnp.tanh(0.5 * x) + 0.5


def _ln(v, g, b):
    mu = jnp.mean(v, axis=-1, keepdims=True)
    c = v - mu
    var = jnp.mean(c * c, axis=-1, keepdims=True)
    return c * lax.rsqrt(var + LN_EPS) * g + b


def _residual_ln_rows(x_ref, acc_ref, g_ref, b_ref, o_ref, bm):
    def body(r, carry):
        rows = pl.ds(pl.multiple_of(r * LN_RB, LN_RB), LN_RB)
        v = acc_ref[rows, :] if x_ref is None else ALPHA * x_ref[rows, :] + acc_ref[rows, :]
        o_ref[rows, :] = _ln(v, g_ref[...], b_ref[...])
        return carry
    lax.fori_loop(0, bm // LN_RB, body, 0, unroll=LN_UNROLL)


def _scan8(a, u):
    n, cols = a.shape
    a = a.reshape(n // SUBLANES, SUBLANES, cols)
    u = u.reshape(n // SUBLANES, SUBLANES, cols)
    rows = lax.broadcasted_iota(jnp.int32, a.shape, 1)
    for d in (1, 2, 4):
        m = rows >= d
        a_sh = pltpu.roll(a, d, 1)
        u_sh = pltpu.roll(u, d, 1)
        u = jnp.where(m, a * u_sh + u, u)
        a = jnp.where(m, a * a_sh, a)
    return a.reshape(n, cols), u.reshape(n, cols)


def _gate_dots(cb, wr_ref, wi_ref, c):
    hpc = CHUNK // LRU_HEAD_DIM
    rs, gs = [], []
    for hh in range(hpc):
        cbh = cb[:, hh * LRU_HEAD_DIM:(hh + 1) * LRU_HEAD_DIM].astype(BF16)
        rs.append(_dot(cbh, wr_ref[c * hpc + hh]))
        gs.append(_dot(cbh, wi_ref[c * hpc + hh]))
    return jnp.concatenate(rs, axis=-1), jnp.concatenate(gs, axis=-1)


def _lru_coeffs(cb, r_pre, g_pre, br_ref, bi_ref, lam_ref, c, first_tile):
    cols = slice(c * CHUNK, (c + 1) * CHUNK)
    r = _sigmoid(r_pre + br_ref[:, cols])
    gi = _sigmoid(g_pre + bi_ref[:, cols])
    log_a = (-LRU_C) * r * jax.nn.softplus(-lam_ref[:, cols])
    a = jnp.exp(log_a)
    mult = jnp.sqrt(1.0 - a * a)
    if first_tile is not None:
        row0 = lax.broadcasted_iota(jnp.int32, mult.shape, 0) == 0
        mult = jnp.where(jnp.logical_and(row0, first_tile), 1.0, mult)
    return a, mult * gi * cb


def _conv31_taps():
    taps = []
    for r in range(SUBLANES):
        for q in range(HIST_A // SUBLANES):
            j = SUBLANES * q + r
            if j < CONV_A_WIDTH:
                taps.append((r, q, CONV_A_WIDTH - 1 - j))
    return taps


def _conv31_prompt(u_scr, cw_ref, cb_ref, ca_scr, bm):
    def body(blk, carry):
        base = pl.multiple_of(blk * RB, RB)
        for lc in range(D_CONV // LC):
            lanes = pl.ds(lc * LC, LC)
            win = u_scr[pl.ds(base, RB + HIST_A), lanes]
            acc = jnp.broadcast_to(cb_ref[:, lanes], (RB, LC))
            v, v_r = win, 0
            for r, q, k in _conv31_taps():
                if r != v_r:
                    v, v_r = pltpu.roll(win, r, 0), r
                off = HIST_A - SUBLANES * q
                acc = acc + cw_ref[k:k + 1, lanes] * v[off:off + RB, :]
            ca_scr[pl.ds(base, RB), lanes] = acc
        return carry
    lax.fori_loop(0, bm // RB, body, 0)


def _ln_silu_rows(ca_scr, lg_ref, lb_ref, ya_ref, bm):
    def body(r, carry):
        rows = pl.ds(pl.multiple_of(r * RB, RB), RB)
        y = _ln(ca_scr[rows, :], lg_ref[...], lb_ref[...])
        ya_ref[rows, :] = (y * _sigmoid(y)).astype(BF16)
        return carry
    lax.fori_loop(0, bm // RB, body, 0, unroll=2)


def _a1_prompt_kernel(x_ref, wv_ref, wg_ref, cw_ref, cb_ref, lg_ref, lb_ref, ya_ref, na_ref, u_scr, ca_scr, *, bm, tps):
    t = pl.program_id(0) % tps

    @pl.when(t == 0)
    def _():
        u_scr[0:HIST_A, :] = jnp.zeros((HIST_A, D_CONV), F32)

    @pl.when(t != 0)
    def _():
        u_scr[0:HIST_A, :] = u_scr[bm:bm + HIST_A, :]

    xb = x_ref[...].astype(BF16)
    for c in range(D_CONV // CHUNK):
        cols = slice(c * CHUNK, (c + 1) * CHUNK)
        av = _dot(xb, wv_ref[:, cols])
        ag = _dot(xb, wg_ref[:, cols])
        u_scr[HIST_A:HIST_A + bm, cols] = av * _sigmoid(ag)
    na_ref[...] = u_scr[HIST_A + bm - (CONV_A_WIDTH - 1):HIST_A + bm, :]
    _conv31_prompt(u_scr, cw_ref, cb_ref, ca_scr, bm)
    _ln_silu_rows(ca_scr, lg_ref, lb_ref, ya_ref, bm)


def _a1_sample_kernel(x_ref, st_ref, wv_ref, wg_ref, cw_ref, cb_ref, lg_ref, lb_ref, ya_ref, na_ref, ext_scr, ca_scr, *, s_n):
    hist = CONV_A_WIDTH - 1
    xb = x_ref[...].reshape(TS * s_n, D_MODEL).astype(BF16)
    ext_scr[0:hist] = st_ref[...]
    for c in range(D_CONV // CHUNK):
        cols = slice(c * CHUNK, (c + 1) * CHUNK)
        av = _dot(xb, wv_ref[:, cols])
        ag = _dot(xb, wg_ref[:, cols])
        ext_scr[hist:hist + TS, :, cols] = (av * _sigmoid(ag)).reshape(TS, s_n, CHUNK)
    na_ref[...] = ext_scr[TS:TS + hist]

    def conv(t, carry):
        for lc in range(D_CONV // CHUNK):
            lanes = pl.ds(lc * CHUNK, CHUNK)
            acc = jnp.broadcast_to(cb_ref[:, lanes], (s_n, CHUNK))
            for k in range(CONV_A_WIDTH):
                acc = acc + cw_ref[k:k + 1, lanes] * ext_scr[t + k, :, lanes]
            ca_scr[t, :, lanes] = acc
        return carry
    lax.fori_loop(0, TS, conv, 0)

    def norm(t, carry):
        y = _ln(ca_scr[t], lg_ref[...], lb_ref[...])
        ya_ref[t] = (y * _sigmoid(y)).astype(BF16)
        return carry
    lax.fori_loop(0, TS, norm, 0, unroll=2)


def _a2_prompt_kernel(x_ref, wbx_ref, wbg_ref, cw_ref, cb_ref, wr_ref, br_ref, wi_ref, bi_ref, lam_ref,
                      yb_ref, nb_ref, hl_ref, bx_scr, a_scr, u_scr, g_scr, hc_scr, *, bm, tps):
    t = pl.program_id(0) % tps
    hist = CONV_B_WIDTH - 1
    nch = D_LRU // CHUNK

    @pl.when(t == 0)
    def _():
        bx_scr[0:HIST, :] = jnp.zeros((HIST, D_LRU), F32)
        hc_scr[...] = jnp.zeros((SUBLANES, D_LRU), F32)

    @pl.when(t != 0)
    def _():
        bx_scr[0:HIST, :] = bx_scr[bm:bm + HIST, :]

    xb = x_ref[...].astype(BF16)
    first_tile = t == 0
    chunk = lambda c: slice(c * CHUNK, (c + 1) * CHUNK)
    bx_next = _dot(xb, wbx_ref[:, chunk(0)])
    for c in range(nch):
        cols = chunk(c)
        bx = bx_next
        bx_scr[HIST:HIST + bm, cols] = bx
        cb = cb_ref[:, cols] + cw_ref[hist:hist + 1, cols] * bx
        for k in range(hist):
            cb = cb + cw_ref[k:k + 1, cols] * bx_scr[HIST - hist + k:HIST - hist + k + bm, cols]
        r_pre, g_pre = _gate_dots(cb, wr_ref, wi_ref, c)
        if c + 1 < nch:
            bx_next = _dot(xb, wbx_ref[:, chunk(c + 1)])
        bg = _dot(xb, wbg_ref[:, cols])
        a, u = _lru_coeffs(cb, r_pre, g_pre, br_ref, bi_ref, lam_ref, c, first_tile)
        a, u = _scan8(a, u)
        a_scr[:, cols] = a
        u_scr[:, cols] = u
        g_scr[:, cols] = jax.nn.gelu(bg)
    nb_ref[...] = bx_scr[HIST + bm - hist:HIST + bm, :]

    def body(r, carry):
        rows = pl.ds(pl.multiple_of(r * 2 * SUBLANES, 2 * SUBLANES), 2 * SUBLANES)
        a2, u2 = a_scr[rows, :], u_scr[rows, :]
        h0 = u2[0:SUBLANES, :] + a2[0:SUBLANES, :] * carry
        mid = jnp.broadcast_to(h0[SUBLANES - 1:SUBLANES, :], (SUBLANES, D_LRU))
        h1 = u2[SUBLANES:, :] + a2[SUBLANES:, :] * mid
        yb_ref[rows, :] = (jnp.concatenate([h0, h1], axis=0) * g_scr[rows, :]).astype(BF16)
        return jnp.broadcast_to(h1[SUBLANES - 1:SUBLANES, :], (SUBLANES, D_LRU))
    carry = lax.fori_loop(0, bm // (2 * SUBLANES), body, hc_scr[...])
    hc_scr[...] = carry
    hl_ref[...] = carry[0:1, :]


def _a2_sample_kernel(x_ref, stb_ref, h0_ref, wbx_ref, wbg_ref, cw_ref, cb_ref, wr_ref, br_ref, wi_ref, bi_ref, lam_ref,
                      yb_ref, nb_ref, hl_ref, *, s_n):
    hist = CONV_B_WIDTH - 1
    xb = x_ref[...].reshape(TS * s_n, D_MODEL).astype(BF16)
    nch = D_LRU // CHUNK
    chunk = lambda c: slice(c * CHUNK, (c + 1) * CHUNK)
    slab = lambda v, t: v[t * s_n:(t + 1) * s_n, :]
    bx_next = _dot(xb, wbx_ref[:, chunk(0)])
    for c in range(nch):
        cols = chunk(c)
        bx = bx_next
        ext = [stb_ref[k, :, cols] for k in range(hist)] + [slab(bx, t) for t in range(TS)]
        cb = jnp.concatenate(
            [cb_ref[:, cols] + sum(cw_ref[k:k + 1, cols] * ext[t + k] for k in range(CONV_B_WIDTH)) for t in range(TS)],
            axis=0)
        r_pre, g_pre = _gate_dots(cb, wr_ref, wi_ref, c)
        if c + 1 < nch:
            bx_next = _dot(xb, wbx_ref[:, chunk(c + 1)])
        bg = _dot(xb, wbg_ref[:, cols])
        a, u = _lru_coeffs(cb, r_pre, g_pre, br_ref, bi_ref, lam_ref, c, None)
        h = h0_ref[:, cols]
        hs = []
        for t in range(TS):
            h = slab(a, t) * h + slab(u, t)
            hs.append(h)
        hl_ref[:, cols] = h
        yb_ref[:, :, cols] = (jnp.concatenate(hs, axis=0) * jax.nn.gelu(bg)).astype(BF16).reshape(TS, s_n, CHUNK)
        for k in range(hist):
            nb_ref[k, :, cols] = slab(bx, TS - hist + k)


def _merge_kernel(x_ref, ya_ref, yb_ref, wga_ref, wgb_ref, wa_ref, wb_ref, wo_ref, g_ref, b_ref, o_ref, xb_scr, acc_scr, *, bm, nc):
    c = pl.program_id(1)

    @pl.when(c == 0)
    def _():
        xb_scr[...] = x_ref[...].astype(BF16)
        acc_scr[...] = jnp.zeros((bm, D_MODEL), F32)

    sb = bm // NSPLIT

    def up(s):
        rows = slice(s * sb, (s + 1) * sb)
        xb = xb_scr[rows, :]
        return (_dot(xb, wga_ref[...]), _dot(ya_ref[rows, :], wa_ref[...]),
                _dot(xb, wgb_ref[...]), _dot(yb_ref[rows, :], wb_ref[...]))

    def down(s, ga, out_a, gb, out_b):
        merged = _sigmoid(ga) * out_a + _sigmoid(gb) * out_b
        acc_scr[s * sb:(s + 1) * sb, :] += _dot(merged.astype(BF16), wo_ref[...])

    pending = up(0)
    for s in range(NSPLIT):
        nxt = up(s + 1) if s + 1 < NSPLIT else None
        down(s, *pending)
        pending = nxt

    @pl.when(c == nc - 1)
    def _():
        _residual_ln_rows(x_ref, acc_scr, g_ref, b_ref, o_ref, bm)


def _ffn_prompt_kernel(x_ref, wfu_ref, wfg_ref, cw_ref, cb_ref, wd_ref, g_ref, b_ref, o_ref, nf_ref,
                       xb_scr, acc_scr, fg_scr, carry_scr, *, bm, tps, nc):
    t = pl.program_id(0) % tps
    c = pl.program_id(1)
    hist = FFN_CONV_WIDTH - 1

    @pl.when(c == 0)
    def _():
        xb_scr[...] = x_ref[...].astype(BF16)
        acc_scr[...] = jnp.zeros((bm, D_MODEL), F32)

    @pl.when(t == 0)
    def _():
        fg_scr[0:HIST, :] = jnp.zeros((HIST, CF), F32)

    @pl.when(t != 0)
    def _():
        fg_scr[0:HIST, :] = carry_scr[c]

    sb = bm // NSPLIT

    def up(s):
        xb = xb_scr[s * sb:(s + 1) * sb, :]
        fg = _dot(xb, wfg_ref[...])
        fg_scr[HIST + s * sb:HIST + (s + 1) * sb, :] = fg
        return _dot(xb, wfu_ref[...]), fg

    def down(s, fu, fg):
        r0 = s * sb
        fgc = cb_ref[...] + cw_ref[hist:hist + 1, :] * fg
        for k in range(hist):
            fgc = fgc + cw_ref[k:k + 1, :] * fg_scr[HIST - hist + k + r0:HIST - hist + k + r0 + sb, :]
        hf = jax.nn.gelu(fgc) * fu
        acc_scr[r0:r0 + sb, :] += _dot(hf.astype(BF16), wd_ref[...])

    pending = up(0)
    for s in range(NSPLIT):
        nxt = up(s + 1) if s + 1 < NSPLIT else None
        down(s, *pending)
        pending = nxt
    carry_scr[c] = fg_scr[bm:bm + HIST, :]
    nf_ref[...] = fg_scr[HIST + bm - hist:HIST + bm, :]

    @pl.when(c == nc - 1)
    def _():
        _residual_ln_rows(x_ref, acc_scr, g_ref, b_ref, o_ref, bm)


def _ffn_sample_kernel(x_ref, st_ref, wfu_ref, wfg_ref, cw_ref, cb_ref, wd_ref, g_ref, b_ref,
                       o_ref, nf_ref, wfub_ref, wfgb_ref, wdb_ref, xb_scr, *, s_n, nc):
    c = pl.program_id(0)
    hist = FFN_CONV_WIDTH - 1
    rows = TS * s_n

    @pl.when(c == 0)
    def _():
        xb_scr[...] = x_ref[...].astype(BF16)
        o_ref[...] = ALPHA * x_ref[...]

    wfub_ref[...] = wfu_ref[...].astype(BF16)
    wfgb_ref[...] = wfg_ref[...].astype(BF16)
    wdb_ref[...] = wd_ref[...].astype(BF16)

    spt = TS // NSPLIT_S
    sb = spt * s_n

    def up(j):
        xb = xb_scr[j * sb:(j + 1) * sb, :]
        return _dot(xb, wfub_ref[...]), _dot(xb, wfgb_ref[...])

    gate = [st_ref[k] for k in range(hist)]
    pending = up(0)
    for j in range(NSPLIT_S):
        nxt = up(j + 1) if j + 1 < NSPLIT_S else None
        fu, fg = pending
        gate += [fg[q * s_n:(q + 1) * s_n, :] for q in range(spt)]
        fgc = jnp.concatenate(
            [cb_ref[...] + sum(cw_ref[k:k + 1, :] * gate[j * spt + q + k] for k in range(FFN_CONV_WIDTH))
             for q in range(spt)], axis=0)
        hf = jax.nn.gelu(fgc) * fu
        o_ref[j * sb:(j + 1) * sb, :] += _dot(hf.astype(BF16), wdb_ref[...])
        pending = nxt
    for k in range(hist):
        nf_ref[k] = gate[TS + k]

    @pl.when(c == nc - 1)
    def _():
        _residual_ln_rows(None, o_ref, g_ref, b_ref, o_ref, rows)


def _ple_kernel(x_ref, p_ref, wpg_ref, bpg_ref, wpe_ref, g_ref, b_ref, o_ref, e_scr, *, bm):
    xb = x_ref[...].astype(BF16)
    pb = p_ref[...].astype(BF16)
    for c in range(D_MODEL // CHUNK):
        cols = slice(c * CHUNK, (c + 1) * CHUNK)
        gate = _sigmoid(_dot(xb, wpg_ref[:, cols]) + bpg_ref[:, cols])
        e_scr[:, cols] = gate * _dot(pb, wpe_ref[:, cols])
    _residual_ln_rows(x_ref, e_scr, g_ref, b_ref, o_ref, bm)


def _params(n_axes):
    return pltpu.CompilerParams(dimension_semantics=("arbitrary",) * n_axes, vmem_limit_bytes=VMEM_LIMIT_BYTES)


def _resident(shape, index_map):
    return pl.BlockSpec(shape, index_map, pipeline_mode=pl.Buffered(1))


def _row(l, width):
    return _resident((None, 1, width), lambda *idx: (l, 0, 0))


def _mixer_a(l, x, state, w_in_b, conv_w, conv_b, ln_g, ln_b, prompt, seqs):
    hist = CONV_A_WIDTH - 1
    weights = [
        _resident((None, D_MODEL, D_CONV), lambda i: (l, 0, OFF_AV // D_CONV)),
        _resident((None, D_MODEL, D_CONV), lambda i: (l, 0, OFF_AG // D_CONV)),
        _resident((None, CONV_A_WIDTH, D_CONV), lambda i: (l, 0, 0)),
        _row(l, D_CONV), _row(l, D_CONV), _row(l, D_CONV),
    ]
    args = (w_in_b, w_in_b, conv_w, conv_b, ln_g, ln_b)
    if prompt:
        n, bm = x.shape[0], BM_PROMPT
        tps = (n // seqs) // bm
        return pl.pallas_call(
            functools.partial(_a1_prompt_kernel, bm=bm, tps=tps),
            grid=(n // bm,),
            in_specs=[pl.BlockSpec((bm, D_MODEL), lambda i: (i, 0))] + weights,
            out_specs=[pl.BlockSpec((bm, D_CONV), lambda i: (i, 0)),
                       pl.BlockSpec((None, hist, D_CONV), lambda i: (i // tps, 0, 0))],
            out_shape=[jax.ShapeDtypeStruct((n, D_CONV), BF16), jax.ShapeDtypeStruct((seqs, hist, D_CONV), F32)],
            scratch_shapes=[pltpu.VMEM((HIST_A + bm, D_CONV), F32), pltpu.VMEM((bm, D_CONV), F32)],
            compiler_params=_params(1), name=f"mixer_a_prompt_l{l}",
        )(x, *args)
    s_n = S_MIX
    return pl.pallas_call(
        functools.partial(_a1_sample_kernel, s_n=s_n),
        grid=(seqs // s_n,),
        in_specs=[pl.BlockSpec((TS, s_n, D_MODEL), lambda i: (0, i, 0)),
                  pl.BlockSpec((None, hist, s_n, D_CONV), lambda i: (l, 0, i, 0))] + weights,
        out_specs=[pl.BlockSpec((TS, s_n, D_CONV), lambda i: (0, i, 0)),
                   pl.BlockSpec((hist, s_n, D_CONV), lambda i: (0, i, 0))],
        out_shape=[jax.ShapeDtypeStruct((TS, seqs, D_CONV), BF16), jax.ShapeDtypeStruct((hist, seqs, D_CONV), F32)],
        scratch_shapes=[pltpu.VMEM((hist + TS, s_n, D_CONV), F32), pltpu.VMEM((TS, s_n, D_CONV), F32)],
        compiler_params=_params(1), name=f"mixer_a_sample_l{l}",
    )(x, state, *args)


def _mixer_b(l, x, state_b, state_h, w_in_b, conv_w, conv_b, w_r_b, b_r, w_i_b, b_i, lam, prompt, seqs):
    bm = BM_MIXER_B
    hist = CONV_B_WIDTH - 1
    gate_w = _resident((None, LRU_HEADS, LRU_HEAD_DIM, LRU_HEAD_DIM), lambda i: (l, 0, 0, 0))
    weights = [
        _resident((None, D_MODEL, D_LRU), lambda i: (l, 0, OFF_BX // D_LRU)),
        _resident((None, D_MODEL, D_LRU), lambda i: (l, 0, OFF_BG // D_LRU)),
        _resident((None, CONV_B_WIDTH, D_LRU), lambda i: (l, 0, 0)),
        _row(l, D_LRU), gate_w, _row(l, D_LRU), gate_w, _row(l, D_LRU), _row(l, D_LRU),
    ]
    args = (w_in_b, w_in_b, conv_w, conv_b, w_r_b, b_r, w_i_b, b_i, lam)
    if prompt:
        n = x.shape[0]
        tps = (n // seqs) // bm
        yb, nb, hl = pl.pallas_call(
            functools.partial(_a2_prompt_kernel, bm=bm, tps=tps),
            grid=(n // bm,),
            in_specs=[pl.BlockSpec((bm, D_MODEL), lambda i: (i, 0))] + weights,
            out_specs=[pl.BlockSpec((bm, D_LRU), lambda i: (i, 0)),
                       pl.BlockSpec((None, hist, D_LRU), lambda i: (i // tps, 0, 0)),
                       pl.BlockSpec((None, 1, D_LRU), lambda i: (i // tps, 0, 0))],
            out_shape=[jax.ShapeDtypeStruct((n, D_LRU), BF16),
                       jax.ShapeDtypeStruct((seqs, hist, D_LRU), F32),
                       jax.ShapeDtypeStruct((seqs, 1, D_LRU), F32)],
            scratch_shapes=[pltpu.VMEM((HIST + bm, D_LRU), F32), pltpu.VMEM((bm, D_LRU), F32),
                            pltpu.VMEM((bm, D_LRU), F32), pltpu.VMEM((bm, D_LRU), F32),
                            pltpu.VMEM((SUBLANES, D_LRU), F32)],
            compiler_params=_params(1), name=f"mixer_b_prompt_l{l}",
        )(x, *args)
        return yb, nb, hl.reshape(seqs, D_LRU)
    s_n = S_MIX
    return pl.pallas_call(
        functools.partial(_a2_sample_kernel, s_n=s_n),
        grid=(seqs // s_n,),
        in_specs=[pl.BlockSpec((TS, s_n, D_MODEL), lambda i: (0, i, 0)),
                  pl.BlockSpec((None, hist, s_n, D_LRU), lambda i: (l, 0, i, 0)),
                  pl.BlockSpec((None, s_n, D_LRU), lambda i: (l, i, 0))] + weights,
        out_specs=[pl.BlockSpec((TS, s_n, D_LRU), lambda i: (0, i, 0)),
                   pl.BlockSpec((hist, s_n, D_LRU), lambda i: (0, i, 0)),
                   pl.BlockSpec((s_n, D_LRU), lambda i: (i, 0))],
        out_shape=[jax.ShapeDtypeStruct((TS, seqs, D_LRU), BF16),
                   jax.ShapeDtypeStruct((hist, seqs, D_LRU), F32),
                   jax.ShapeDtypeStruct((seqs, D_LRU), F32)],
        compiler_params=_params(1), name=f"mixer_b_sample_l{l}",
    )(x, state_b, state_h, *args)


def _merge(l, x2, ya, yb, w_in_b, w_a_out_b, w_b_out_b, w_o_b, g, b, prompt):
    n = x2.shape[0]
    bm = BM_PROMPT if prompt else BM_SAMPLE
    nc = D_MODEL // CM
    return pl.pallas_call(
        functools.partial(_merge_kernel, bm=bm, nc=nc),
        grid=(n // bm, nc),
        in_specs=[
            pl.BlockSpec((bm, D_MODEL), lambda i, c: (i, 0)),
            pl.BlockSpec((bm, D_CONV), lambda i, c: (i, 0)),
            pl.BlockSpec((bm, D_LRU), lambda i, c: (i, 0)),
            pl.BlockSpec((None, D_MODEL, CM), lambda i, c: (l, 0, OFF_GA // CM + c)),
            pl.BlockSpec((None, D_MODEL, CM), lambda i, c: (l, 0, OFF_GB // CM + c)),
            pl.BlockSpec((None, D_CONV, CM), lambda i, c: (l, 0, c)),
            pl.BlockSpec((None, D_LRU, CM), lambda i, c: (l, 0, c)),
            pl.BlockSpec((None, CM, D_MODEL), lambda i, c: (l, c, 0)),
            _row(l, D_MODEL), _row(l, D_MODEL),
        ],
        out_specs=pl.BlockSpec((bm, D_MODEL), lambda i, c: (i, 0)),
        out_shape=jax.ShapeDtypeStruct((n, D_MODEL), F32),
        scratch_shapes=[pltpu.VMEM((bm, D_MODEL), BF16), pltpu.VMEM((bm, D_MODEL), F32)],
        compiler_params=_params(2), name=f"merge_{'prompt' if prompt else 'sample'}_l{l}",
    )(x2, ya, yb, w_in_b, w_in_b, w_a_out_b, w_b_out_b, w_o_b, g, b)


def _ffn_sample(l, x2, state, w_up, conv_w, conv_b, w_down, g, b, seqs):
    n = x2.shape[0]
    nc = D_FF // CF_S
    hist = FFN_CONV_WIDTH - 1
    return pl.pallas_call(
        functools.partial(_ffn_sample_kernel, s_n=seqs, nc=nc),
        grid=(nc,),
        in_specs=[
            _resident((n, D_MODEL), lambda c: (0, 0)),
            pl.BlockSpec((None, hist, seqs, CF_S), lambda c: (l, 0, 0, c)),
            pl.BlockSpec((None, D_MODEL, CF_S), lambda c: (l, 0, c)),
            pl.BlockSpec((None, D_MODEL, CF_S), lambda c: (l, 0, nc + c)),
            pl.BlockSpec((None, FFN_CONV_WIDTH, CF_S), lambda c: (l, 0, c)),
            pl.BlockSpec((None, 1, CF_S), lambda c: (l, 0, c)),
            pl.BlockSpec((None, CF_S, D_MODEL), lambda c: (l, c, 0)),
            _row(l, D_MODEL), _row(l, D_MODEL),
        ],
        out_specs=[
            pl.BlockSpec((n, D_MODEL), lambda c: (0, 0)),
            pl.BlockSpec((hist, seqs, CF_S), lambda c: (0, 0, c)),
            pl.BlockSpec((D_MODEL, CF_S), lambda c: (0, c)),
            pl.BlockSpec((D_MODEL, CF_S), lambda c: (0, c)),
            pl.BlockSpec((CF_S, D_MODEL), lambda c: (c, 0)),
        ],
        out_shape=[
            jax.ShapeDtypeStruct((n, D_MODEL), F32),
            jax.ShapeDtypeStruct((hist, seqs, D_FF), F32),
            jax.ShapeDtypeStruct((D_MODEL, D_FF), BF16),
            jax.ShapeDtypeStruct((D_MODEL, D_FF), BF16),
            jax.ShapeDtypeStruct((D_FF, D_MODEL), BF16),
        ],
        scratch_shapes=[pltpu.VMEM((n, D_MODEL), BF16)],
        compiler_params=_params(1), name=f"ffn_sample_l{l}",
    )(x2, state, w_up, w_up, conv_w, conv_b, w_down, g, b)


def _ffn_prompt(l, x2, wfu_b, wfg_b, conv_w, conv_b, wd_b, g, b, seqs):
    n = x2.shape[0]
    bm = BM_PROMPT
    nc = D_FF // CF
    hist = FFN_CONV_WIDTH - 1
    tps = (n // seqs) // bm
    x_out, nf_tiles = pl.pallas_call(
        functools.partial(_ffn_prompt_kernel, bm=bm, tps=tps, nc=nc),
        grid=(n // bm, nc),
        in_specs=[
            pl.BlockSpec((bm, D_MODEL), lambda i, c: (i, 0)),
            pl.BlockSpec((D_MODEL, CF), lambda i, c: (0, c)),
            pl.BlockSpec((D_MODEL, CF), lambda i, c: (0, c)),
            pl.BlockSpec((None, FFN_CONV_WIDTH, CF), lambda i, c: (l, 0, c)),
            pl.BlockSpec((None, 1, CF), lambda i, c: (l, 0, c)),
            pl.BlockSpec((CF, D_MODEL), lambda i, c: (c, 0)),
            _row(l, D_MODEL), _row(l, D_MODEL),
        ],
        out_specs=[pl.BlockSpec((bm, D_MODEL), lambda i, c: (i, 0)),
                   pl.BlockSpec((None, hist, CF), lambda i, c: (i, 0, c))],
        out_shape=[jax.ShapeDtypeStruct((n, D_MODEL), F32), jax.ShapeDtypeStruct((n // bm, hist, D_FF), F32)],
        scratch_shapes=[pltpu.VMEM((bm, D_MODEL), BF16), pltpu.VMEM((bm, D_MODEL), F32),
                        pltpu.VMEM((HIST + bm, CF), F32), pltpu.VMEM((nc, HIST, CF), F32)],
        compiler_params=_params(2), name=f"ffn_prompt_l{l}",
    )(x2, wfu_b, wfg_b, conv_w, conv_b, wd_b, g, b)
    return x_out, nf_tiles.reshape(seqs, tps, hist, D_FF)[:, tps - 1]


def _ple(l, x2, p2, w_pg_b, b_pg, w_pe_b, g, b, prompt):
    n = x2.shape[0]
    bm = BM_PROMPT if prompt else BM_SAMPLE
    return pl.pallas_call(
        functools.partial(_ple_kernel, bm=bm),
        grid=(n // bm,),
        in_specs=[
            pl.BlockSpec((bm, D_MODEL), lambda i: (i, 0)),
            pl.BlockSpec((None, bm, D_PLE), lambda i: (l, i, 0)),
            _resident((None, D_MODEL, D_MODEL), lambda i: (l, 0, 0)),
            _row(l, D_MODEL),
            _resident((None, D_PLE, D_MODEL), lambda i: (l, 0, 0)),
            _row(l, D_MODEL), _row(l, D_MODEL),
        ],
        out_specs=pl.BlockSpec((bm, D_MODEL), lambda i: (i, 0)),
        out_shape=jax.ShapeDtypeStruct((n, D_MODEL), F32),
        scratch_shapes=[pltpu.VMEM((bm, D_MODEL), F32)],
        compiler_params=_params(1), name=f"ple_{'prompt' if prompt else 'sample'}_l{l}",
    )(x2, p2, w_pg_b, b_pg, w_pe_b, g, b)


def kernel(x_prompt, x_sample, state_conv_a, state_conv_b, state_rglru, state_conv_ffn, p_prompt, p_sample, w_in, conv_a_w, conv_a_b, ln_a_g, ln_a_b, w_a_out, conv_b_w, conv_b_b, w_r, b_r, w_i, b_i, lru_lambda, w_b_out, w_o, ln1_g, ln1_b, w_up, ffn_conv_w, ffn_conv_b, w_down, ln2_g, ln2_b, w_pe, w_pg, b_pg, ln3_g, ln3_b):
    bp, tp, _ = x_prompt.shape
    bs, ts, _ = x_sample.shape
    assert ts == TS and tp % BM_PROMPT == 0 and tp % BM_MIXER_B == 0
    assert bs % S_MIX == 0 and (bs * ts) % BM_SAMPLE == 0 and TS % NSPLIT_S == 0
    assert w_in.shape == (DEPTH, D_MODEL, OFF_GB + D_MODEL)

    w_in_b, w_a_out_b, w_b_out_b, w_o_b = (w.astype(BF16) for w in (w_in, w_a_out, w_b_out, w_o))
    w_pe_b, w_pg_b, w_r_b, w_i_b = (w.astype(BF16) for w in (w_pe, w_pg, w_r, w_i))
    row = lambda v: v.reshape(DEPTH, 1, v.shape[-1])
    conv_a_b, ln_a_g, ln_a_b, conv_b_b, b_r, b_i, lru_lambda = map(row, (conv_a_b, ln_a_g, ln_a_b, conv_b_b, b_r, b_i, lru_lambda))
    ln1_g, ln1_b, ffn_conv_b, ln2_g, ln2_b, b_pg, ln3_g, ln3_b = map(row, (ln1_g, ln1_b, ffn_conv_b, ln2_g, ln2_b, b_pg, ln3_g, ln3_b))

    xp = x_prompt.reshape(bp * tp, D_MODEL)
    pp = p_prompt.reshape(DEPTH, bp * tp, D_PLE)
    xs = jnp.transpose(x_sample, (1, 0, 2))
    ps = jnp.transpose(p_sample, (0, 2, 1, 3)).reshape(DEPTH, ts * bs, D_PLE)
    sa, sb, sf = (jnp.transpose(s, (0, 2, 1, 3)) for s in (state_conv_a, state_conv_b, state_conv_ffn))
    flat = lambda v: v.reshape(ts * bs, v.shape[-1])

    prompt_states, sample_states = [], []
    for l in range(DEPTH):
        ya, new_a = _mixer_a(l, xs, sa, w_in_b, conv_a_w, conv_a_b, ln_a_g, ln_a_b, False, bs)
        yb, new_b, h_last = _mixer_b(l, xs, sb, state_rglru, w_in_b, conv_b_w, conv_b_b,
                                     w_r_b, b_r, w_i_b, b_i, lru_lambda, False, bs)
        x2 = _merge(l, flat(xs), flat(ya), flat(yb), w_in_b, w_a_out_b, w_b_out_b, w_o_b, ln1_g, ln1_b, False)
        x2, new_f, wfu_b, wfg_b, wd_b = _ffn_sample(l, x2, sf, w_up, ffn_conv_w, ffn_conv_b, w_down, ln2_g, ln2_b, bs)
        xs = _ple(l, x2, ps, w_pg_b, b_pg, w_pe_b, ln3_g, ln3_b, False).reshape(ts, bs, D_MODEL)
        sample_states.append((new_a, new_b, h_last, new_f))

        ya, new_a = _mixer_a(l, xp, None, w_in_b, conv_a_w, conv_a_b, ln_a_g, ln_a_b, True, bp)
        yb, new_b, h_last = _mixer_b(l, xp, None, None, w_in_b, conv_b_w, conv_b_b,
                                     w_r_b, b_r, w_i_b, b_i, lru_lambda, True, bp)
        xp = _merge(l, xp, ya, yb, w_in_b, w_a_out_b, w_b_out_b, w_o_b, ln1_g, ln1_b, True)
        xp, new_f = _ffn_prompt(l, xp, wfu_b, wfg_b, ffn_conv_w, ffn_conv_b, wd_b, ln2_g, ln2_b, bp)
        xp = _ple(l, xp, pp, w_pg_b, b_pg, w_pe_b, ln3_g, ln3_b, True)
        prompt_states.append((new_a, new_b, h_last, new_f))

    stack = lambda states, k: jnp.stack([s[k] for s in states])
    seq_major = lambda v: jnp.transpose(v, (0, 2, 1, 3))
    return (xp.reshape(bp, tp, D_MODEL), jnp.transpose(xs, (1, 0, 2)),
            stack(prompt_states, 0), stack(prompt_states, 1), stack(prompt_states, 2), stack(prompt_states, 3),
            seq_major(stack(sample_states, 0)), seq_major(stack(sample_states, 1)), stack(sample_states, 2),
            seq_major(stack(sample_states, 3)))
```

```python
import functools

import jax
import jax.numpy as jnp
from jax import lax
from jax.experimental import pallas as pl
from jax.experimental.pallas import tpu as pltpu

F32 = jnp.float32
BF16 = jnp.bfloat16

D_MODEL = 2048
DEPTH = 2
D_CONV = D_MODEL // 2
CONV_A_WIDTH = 31
D_LRU = D_MODEL
LRU_HEADS = 16
LRU_HEAD_DIM = D_LRU // LRU_HEADS
CONV_B_WIDTH = 4
LRU_C = 8.0
D_FF = 3 * D_MODEL
FFN_CONV_WIDTH = 3
D_PLE = 256
LN_EPS = 1e-5
ALPHA = (2.0 * DEPTH) ** 0.25
OFF_AV, OFF_AG, OFF_BX, OFF_BG, OFF_GA, OFF_GB = 0, D_CONV, 2 * D_CONV, 2 * D_CONV + D_LRU, 2 * D_CONV + 2 * D_LRU, 2 * D_CONV + 2 * D_LRU + D_MODEL

SUBLANES = 8
VMEM_LIMIT_BYTES = 56 * 1024 * 1024

BM_PROMPT = 512
BM_SAMPLE = 512
BM_MIXER_B = 256
CHUNK = 512
CM = 512
CF = 1024
NSPLIT = 2
TS = 8
S_MIX_A = 32
S_MIX_B = 64
CF_S = 256
NSPLIT_S = 4
RB = 128
LN_RB = 32
LN_UNROLL = 8
LC = 128
HIST_A = 32
HIST = SUBLANES


def _dot(a, b):
    return jnp.dot(a, b, preferred_element_type=F32)


def _sigmoid(x):
    return 0.5 * jnp.tanh(0.5 * x) + 0.5


def _ln(v, g, b):
    mu = jnp.mean(v, axis=-1, keepdims=True)
    c = v - mu
    var = jnp.mean(c * c, axis=-1, keepdims=True)
    return c * lax.rsqrt(var + LN_EPS) * g + b


def _residual_ln_rows(x_ref, acc_ref, g_ref, b_ref, o_ref, bm):
    def body(r, carry):
        rows = pl.ds(pl.multiple_of(r * LN_RB, LN_RB), LN_RB)
        v = acc_ref[rows, :] if x_ref is None else ALPHA * x_ref[rows, :] + acc_ref[rows, :]
        o_ref[rows, :] = _ln(v, g_ref[...], b_ref[...])
        return carry
    lax.fori_loop(0, bm // LN_RB, body, 0, unroll=LN_UNROLL)


def _scan8(a, u):
    n, cols = a.shape
    a = a.reshape(n // SUBLANES, SUBLANES, cols)
    u = u.reshape(n // SUBLANES, SUBLANES, cols)
    rows = lax.broadcasted_iota(jnp.int32, a.shape, 1)
    for d in (1, 2, 4):
        m = rows >= d
        a_sh = pltpu.roll(a, d, 1)
        u_sh = pltpu.roll(u, d, 1)
        u = jnp.where(m, a * u_sh + u, u)
        a = jnp.where(m, a * a_sh, a)
    return a.reshape(n, cols), u.reshape(n, cols)


def _gate_dots(cb, wr_ref, wi_ref, c):
    hpc = CHUNK // LRU_HEAD_DIM
    rs, gs = [], []
    for hh in range(hpc):
        cbh = cb[:, hh * LRU_HEAD_DIM:(hh + 1) * LRU_HEAD_DIM].astype(BF16)
        rs.append(_dot(cbh, wr_ref[c * hpc + hh]))
        gs.append(_dot(cbh, wi_ref[c * hpc + hh]))
    return jnp.concatenate(rs, axis=-1), jnp.concatenate(gs, axis=-1)


def _lru_coeffs(cb, r_pre, g_pre, br_ref, bi_ref, lam_ref, c, first_tile):
    cols = slice(c * CHUNK, (c + 1) * CHUNK)
    r = _sigmoid(r_pre + br_ref[:, cols])
    gi = _sigmoid(g_pre + bi_ref[:, cols])
    log_a = (-LRU_C) * r * jax.nn.softplus(-lam_ref[:, cols])
    a = jnp.exp(log_a)
    mult = jnp.sqrt(1.0 - a * a)
    if first_tile is not None:
        row0 = lax.broadcasted_iota(jnp.int32, mult.shape, 0) == 0
        mult = jnp.where(jnp.logical_and(row0, first_tile), 1.0, mult)
    return a, mult * gi * cb


def _conv31_taps():
    taps = []
    for r in range(SUBLANES):
        for q in range(HIST_A // SUBLANES):
            j = SUBLANES * q + r
            if j < CONV_A_WIDTH:
                taps.append((r, q, CONV_A_WIDTH - 1 - j))
    return taps


def _conv31_prompt(u_scr, cw_ref, cb_ref, ca_scr, bm):
    def body(blk, carry):
        base = pl.multiple_of(blk * RB, RB)
        for lc in range(D_CONV // LC):
            lanes = pl.ds(lc * LC, LC)
            win = u_scr[pl.ds(base, RB + HIST_A), lanes]
            acc = jnp.broadcast_to(cb_ref[:, lanes], (RB, LC))
            v, v_r = win, 0
            for r, q, k in _conv31_taps():
                if r != v_r:
                    v, v_r = pltpu.roll(win, r, 0), r
                off = HIST_A - SUBLANES * q
                acc = acc + cw_ref[k:k + 1, lanes] * v[off:off + RB, :]
            ca_scr[pl.ds(base, RB), lanes] = acc
        return carry
    lax.fori_loop(0, bm // RB, body, 0)


def _ln_silu_rows(ca_scr, lg_ref, lb_ref, ya_ref, bm):
    def body(r, carry):
        rows = pl.ds(pl.multiple_of(r * RB, RB), RB)
        y = _ln(ca_scr[rows, :], lg_ref[...], lb_ref[...])
        ya_ref[rows, :] = (y * _sigmoid(y)).astype(BF16)
        return carry
    lax.fori_loop(0, bm // RB, body, 0, unroll=2)


def _a1_prompt_kernel(x_ref, wv_ref, wg_ref, cw_ref, cb_ref, lg_ref, lb_ref, ya_ref, na_ref, u_scr, ca_scr, *, bm, tps):
    t = pl.program_id(0) % tps

    @pl.when(t == 0)
    def _():
        u_scr[0:HIST_A, :] = jnp.zeros((HIST_A, D_CONV), F32)

    @pl.when(t != 0)
    def _():
        u_scr[0:HIST_A, :] = u_scr[bm:bm + HIST_A, :]

    xb = x_ref[...].astype(BF16)
    for c in range(D_CONV // CHUNK):
        cols = slice(c * CHUNK, (c + 1) * CHUNK)
        av = _dot(xb, wv_ref[:, cols])
        ag = _dot(xb, wg_ref[:, cols])
        u_scr[HIST_A:HIST_A + bm, cols] = av * _sigmoid(ag)
    na_ref[...] = u_scr[HIST_A + bm - (CONV_A_WIDTH - 1):HIST_A + bm, :]
    _conv31_prompt(u_scr, cw_ref, cb_ref, ca_scr, bm)
    _ln_silu_rows(ca_scr, lg_ref, lb_ref, ya_ref, bm)


def _a1_sample_kernel(x_ref, st_ref, wv_ref, wg_ref, cw_ref, cb_ref, lg_ref, lb_ref, ya_ref, na_ref, ext_scr, ca_scr, *, s_n):
    hist = CONV_A_WIDTH - 1
    xb = x_ref[...].reshape(TS * s_n, D_MODEL).astype(BF16)
    ext_scr[0:hist] = st_ref[...]
    for c in range(D_CONV // CHUNK):
        cols = slice(c * CHUNK, (c + 1) * CHUNK)
        av = _dot(xb, wv_ref[:, cols])
        ag = _dot(xb, wg_ref[:, cols])
        ext_scr[hist:hist + TS, :, cols] = (av * _sigmoid(ag)).reshape(TS, s_n, CHUNK)
    na_ref[...] = ext_scr[TS:TS + hist]

    def conv(t, carry):
        for lc in range(D_CONV // CHUNK):
            lanes = pl.ds(lc * CHUNK, CHUNK)
            acc = jnp.broadcast_to(cb_ref[:, lanes], (s_n, CHUNK))
            for k in range(CONV_A_WIDTH):
                acc = acc + cw_ref[k:k + 1, lanes] * ext_scr[t + k, :, lanes]
            ca_scr[t, :, lanes] = acc
        return carry
    lax.fori_loop(0, TS, conv, 0)

    def norm(t, carry):
        y = _ln(ca_scr[t], lg_ref[...], lb_ref[...])
        ya_ref[t] = (y * _sigmoid(y)).astype(BF16)
        return carry
    lax.fori_loop(0, TS, norm, 0, unroll=2)


def _a2_prompt_kernel(x_ref, wbx_ref, wbg_ref, cw_ref, cb_ref, wr_ref, br_ref, wi_ref, bi_ref, lam_ref,
                      yb_ref, nb_ref, hl_ref, bx_scr, a_scr, u_scr, g_scr, hc_scr, *, bm, tps):
    t = pl.program_id(0) % tps
    hist = CONV_B_WIDTH - 1
    nch = D_LRU // CHUNK

    @pl.when(t == 0)
    def _():
        bx_scr[0:HIST, :] = jnp.zeros((HIST, D_LRU), F32)
        hc_scr[...] = jnp.zeros((SUBLANES, D_LRU), F32)

    @pl.when(t != 0)
    def _():
        bx_scr[0:HIST, :] = bx_scr[bm:bm + HIST, :]

    xb = x_ref[...].astype(BF16)
    first_tile = t == 0
    chunk = lambda c: slice(c * CHUNK, (c + 1) * CHUNK)
    bx_next = _dot(xb, wbx_ref[:, chunk(0)])
    for c in range(nch):
        cols = chunk(c)
        bx = bx_next
        bx_scr[HIST:HIST + bm, cols] = bx
        cb = cb_ref[:, cols] + cw_ref[hist:hist + 1, cols] * bx
        for k in range(hist):
            cb = cb + cw_ref[k:k + 1, cols] * bx_scr[HIST - hist + k:HIST - hist + k + bm, cols]
        r_pre, g_pre = _gate_dots(cb, wr_ref, wi_ref, c)
        if c + 1 < nch:
            bx_next = _dot(xb, wbx_ref[:, chunk(c + 1)])
        bg = _dot(xb, wbg_ref[:, cols])
        a, u = _lru_coeffs(cb, r_pre, g_pre, br_ref, bi_ref, lam_ref, c, first_tile)
        a, u = _scan8(a, u)
        a_scr[:, cols] = a
        u_scr[:, cols] = u
        g_scr[:, cols] = jax.nn.gelu(bg)
    nb_ref[...] = bx_scr[HIST + bm - hist:HIST + bm, :]

    def body(r, carry):
        rows = pl.ds(pl.multiple_of(r * 2 * SUBLANES, 2 * SUBLANES), 2 * SUBLANES)
        a2, u2 = a_scr[rows, :], u_scr[rows, :]
        h0 = u2[0:SUBLANES, :] + a2[0:SUBLANES, :] * carry
        mid = jnp.broadcast_to(h0[SUBLANES - 1:SUBLANES, :], (SUBLANES, D_LRU))
        h1 = u2[SUBLANES:, :] + a2[SUBLANES:, :] * mid
        yb_ref[rows, :] = (jnp.concatenate([h0, h1], axis=0) * g_scr[rows, :]).astype(BF16)
        return jnp.broadcast_to(h1[SUBLANES - 1:SUBLANES, :], (SUBLANES, D_LRU))
    carry = lax.fori_loop(0, bm // (2 * SUBLANES), body, hc_scr[...])
    hc_scr[...] = carry
    hl_ref[...] = carry[0:1, :]


def _a2_sample_kernel(x_ref, stb_ref, h0_ref, wbx_ref, wbg_ref, cw_ref, cb_ref, wr_ref, br_ref, wi_ref, bi_ref, lam_ref,
                      yb_ref, nb_ref, hl_ref, *, s_n):
    hist = CONV_B_WIDTH - 1
    xb = x_ref[...].reshape(TS * s_n, D_MODEL).astype(BF16)
    nch = D_LRU // CHUNK
    chunk = lambda c: slice(c * CHUNK, (c + 1) * CHUNK)
    slab = lambda v, t: v[t * s_n:(t + 1) * s_n, :]
    bx_next = _dot(xb, wbx_ref[:, chunk(0)])
    for c in range(nch):
        cols = chunk(c)
        bx = bx_next
        ext = [stb_ref[k, :, cols] for k in range(hist)] + [slab(bx, t) for t in range(TS)]
        cb = jnp.concatenate(
            [cb_ref[:, cols] + sum(cw_ref[k:k + 1, cols] * ext[t + k] for k in range(CONV_B_WIDTH)) for t in range(TS)],
            axis=0)
        r_pre, g_pre = _gate_dots(cb, wr_ref, wi_ref, c)
        if c + 1 < nch:
            bx_next = _dot(xb, wbx_ref[:, chunk(c + 1)])
        bg = _dot(xb, wbg_ref[:, cols])
        a, u = _lru_coeffs(cb, r_pre, g_pre, br_ref, bi_ref, lam_ref, c, None)
        h = h0_ref[:, cols]
        hs = []
        for t in range(TS):
            h = slab(a, t) * h + slab(u, t)
            hs.append(h)
        hl_ref[:, cols] = h
        yb_ref[:, :, cols] = (jnp.concatenate(hs, axis=0) * jax.nn.gelu(bg)).astype(BF16).reshape(TS, s_n, CHUNK)
        for k in range(hist):
            nb_ref[k, :, cols] = slab(bx, TS - hist + k)


def _merge_kernel(x_ref, ya_ref, yb_ref, wga_ref, wgb_ref, wa_ref, wb_ref, wo_ref, g_ref, b_ref, o_ref, xb_scr, acc_scr, *, bm, nc):
    c = pl.program_id(1)

    @pl.when(c == 0)
    def _():
        xb_scr[...] = x_ref[...].astype(BF16)
        acc_scr[...] = jnp.zeros((bm, D_MODEL), F32)

    sb = bm // NSPLIT

    def up(s):
        rows = slice(s * sb, (s + 1) * sb)
        xb = xb_scr[rows, :]
        return (_dot(xb, wga_ref[...]), _dot(ya_ref[rows, :], wa_ref[...]),
                _dot(xb, wgb_ref[...]), _dot(yb_ref[rows, :], wb_ref[...]))

    def down(s, ga, out_a, gb, out_b):
        merged = _sigmoid(ga) * out_a + _sigmoid(gb) * out_b
        acc_scr[s * sb:(s + 1) * sb, :] += _dot(merged.astype(BF16), wo_ref[...])

    pending = up(0)
    for s in range(NSPLIT):
        nxt = up(s + 1) if s + 1 < NSPLIT else None
        down(s, *pending)
        pending = nxt

    @pl.when(c == nc - 1)
    def _():
        _residual_ln_rows(x_ref, acc_scr, g_ref, b_ref, o_ref, bm)


def _ffn_prompt_kernel(x_ref, wfu_ref, wfg_ref, cw_ref, cb_ref, wd_ref, g_ref, b_ref, o_ref, nf_ref,
                       xb_scr, acc_scr, fg_scr, carry_scr, *, bm, tps, nc):
    t = pl.program_id(0) % tps
    c = pl.program_id(1)
    hist = FFN_CONV_WIDTH - 1

    @pl.when(c == 0)
    def _():
        xb_scr[...] = x_ref[...].astype(BF16)
        acc_scr[...] = jnp.zeros((bm, D_MODEL), F32)

    @pl.when(t == 0)
    def _():
        fg_scr[0:HIST, :] = jnp.zeros((HIST, CF), F32)

    @pl.when(t != 0)
    def _():
        fg_scr[0:HIST, :] = carry_scr[c]

    sb = bm // NSPLIT

    def up(s):
        xb = xb_scr[s * sb:(s + 1) * sb, :]
        fg = _dot(xb, wfg_ref[...])
        fg_scr[HIST + s * sb:HIST + (s + 1) * sb, :] = fg
        return _dot(xb, wfu_ref[...]), fg

    def down(s, fu, fg):
        r0 = s * sb
        fgc = cb_ref[...] + cw_ref[hist:hist + 1, :] * fg
        for k in range(hist):
            fgc = fgc + cw_ref[k:k + 1, :] * fg_scr[HIST - hist + k + r0:HIST - hist + k + r0 + sb, :]
        hf = jax.nn.gelu(fgc) * fu
        acc_scr[r0:r0 + sb, :] += _dot(hf.astype(BF16), wd_ref[...])

    pending = up(0)
    for s in range(NSPLIT):
        nxt = up(s + 1) if s + 1 < NSPLIT else None
        down(s, *pending)
        pending = nxt
    carry_scr[c] = fg_scr[bm:bm + HIST, :]
    nf_ref[...] = fg_scr[HIST + bm - hist:HIST + bm, :]

    @pl.when(c == nc - 1)
    def _():
        _residual_ln_rows(x_ref, acc_scr, g_ref, b_ref, o_ref, bm)


def _ffn_sample_kernel(x_ref, st_ref, wfu_ref, wfg_ref, cw_ref, cb_ref, wd_ref, g_ref, b_ref,
                       o_ref, nf_ref, wfub_ref, wfgb_ref, wdb_ref, xb_scr, *, s_n, nc):
    c = pl.program_id(0)
    hist = FFN_CONV_WIDTH - 1
    rows = TS * s_n

    @pl.when(c == 0)
    def _():
        xb_scr[...] = x_ref[...].astype(BF16)
        o_ref[...] = ALPHA * x_ref[...]

    wfub_ref[...] = wfu_ref[...].astype(BF16)
    wfgb_ref[...] = wfg_ref[...].astype(BF16)
    wdb_ref[...] = wd_ref[...].astype(BF16)

    spt = TS // NSPLIT_S
    sb = spt * s_n

    def up(j):
        xb = xb_scr[j * sb:(j + 1) * sb, :]
        return _dot(xb, wfub_ref[...]), _dot(xb, wfgb_ref[...])

    gate = [st_ref[k] for k in range(hist)]
    pending = up(0)
    for j in range(NSPLIT_S):
        nxt = up(j + 1) if j + 1 < NSPLIT_S else None
        fu, fg = pending
        gate += [fg[q * s_n:(q + 1) * s_n, :] for q in range(spt)]
        fgc = jnp.concatenate(
            [cb_ref[...] + sum(cw_ref[k:k + 1, :] * gate[j * spt + q + k] for k in range(FFN_CONV_WIDTH))
             for q in range(spt)], axis=0)
        hf = jax.nn.gelu(fgc) * fu
        o_ref[j * sb:(j + 1) * sb, :] += _dot(hf.astype(BF16), wdb_ref[...])
        pending = nxt
    for k in range(hist):
        nf_ref[k] = gate[TS + k]

    @pl.when(c == nc - 1)
    def _():
        _residual_ln_rows(None, o_ref, g_ref, b_ref, o_ref, rows)


def _ple_kernel(x_ref, p_ref, wpg_ref, bpg_ref, wpe_ref, g_ref, b_ref, o_ref, e_scr, *, bm):
    xb = x_ref[...].astype(BF16)
    pb = p_ref[...].astype(BF16)
    for c in range(D_MODEL // CHUNK):
        cols = slice(c * CHUNK, (c + 1) * CHUNK)
        gate = _sigmoid(_dot(xb, wpg_ref[:, cols]) + bpg_ref[:, cols])
        e_scr[:, cols] = gate * _dot(pb, wpe_ref[:, cols])
    _residual_ln_rows(x_ref, e_scr, g_ref, b_ref, o_ref, bm)


def _params(n_axes):
    return pltpu.CompilerParams(dimension_semantics=("arbitrary",) * n_axes, vmem_limit_bytes=VMEM_LIMIT_BYTES)


def _resident(shape, index_map):
    return pl.BlockSpec(shape, index_map, pipeline_mode=pl.Buffered(1))


def _row(l, width):
    return _resident((None, 1, width), lambda *idx: (l, 0, 0))


def _mixer_a(l, x, state, w_in_b, conv_w, conv_b, ln_g, ln_b, prompt, seqs):
    hist = CONV_A_WIDTH - 1
    weights = [
        _resident((None, D_MODEL, D_CONV), lambda i: (l, 0, OFF_AV // D_CONV)),
        _resident((None, D_MODEL, D_CONV), lambda i: (l, 0, OFF_AG // D_CONV)),
        _resident((None, CONV_A_WIDTH, D_CONV), lambda i: (l, 0, 0)),
        _row(l, D_CONV), _row(l, D_CONV), _row(l, D_CONV),
    ]
    args = (w_in_b, w_in_b, conv_w, conv_b, ln_g, ln_b)
    if prompt:
        n, bm = x.shape[0], BM_PROMPT
        tps = (n // seqs) // bm
        return pl.pallas_call(
            functools.partial(_a1_prompt_kernel, bm=bm, tps=tps),
            grid=(n // bm,),
            in_specs=[pl.BlockSpec((bm, D_MODEL), lambda i: (i, 0))] + weights,
            out_specs=[pl.BlockSpec((bm, D_CONV), lambda i: (i, 0)),
                       pl.BlockSpec((None, hist, D_CONV), lambda i: (i // tps, 0, 0))],
            out_shape=[jax.ShapeDtypeStruct((n, D_CONV), BF16), jax.ShapeDtypeStruct((seqs, hist, D_CONV), F32)],
            scratch_shapes=[pltpu.VMEM((HIST_A + bm, D_CONV), F32), pltpu.VMEM((bm, D_CONV), F32)],
            compiler_params=_params(1), name=f"mixer_a_prompt_l{l}",
        )(x, *args)
    s_n = S_MIX_A
    return pl.pallas_call(
        functools.partial(_a1_sample_kernel, s_n=s_n),
        grid=(seqs // s_n,),
        in_specs=[pl.BlockSpec((TS, s_n, D_MODEL), lambda i: (0, i, 0)),
                  pl.BlockSpec((None, hist, s_n, D_CONV), lambda i: (l, 0, i, 0))] + weights,
        out_specs=[pl.BlockSpec((TS, s_n, D_CONV), lambda i: (0, i, 0)),
                   pl.BlockSpec((hist, s_n, D_CONV), lambda i: (0, i, 0))],
        out_shape=[jax.ShapeDtypeStruct((TS, seqs, D_CONV), BF16), jax.ShapeDtypeStruct((hist, seqs, D_CONV), F32)],
        scratch_shapes=[pltpu.VMEM((hist + TS, s_n, D_CONV), F32), pltpu.VMEM((TS, s_n, D_CONV), F32)],
        compiler_params=_params(1), name=f"mixer_a_sample_l{l}",
    )(x, state, *args)


def _mixer_b(l, x, state_b, state_h, w_in_b, conv_w, conv_b, w_r_b, b_r, w_i_b, b_i, lam, prompt, seqs):
    bm = BM_MIXER_B
    hist = CONV_B_WIDTH - 1
    gate_w = _resident((None, LRU_HEADS, LRU_HEAD_DIM, LRU_HEAD_DIM), lambda i: (l, 0, 0, 0))
    weights = [
        _resident((None, D_MODEL, D_LRU), lambda i: (l, 0, OFF_BX // D_LRU)),
        _resident((None, D_MODEL, D_LRU), lambda i: (l, 0, OFF_BG // D_LRU)),
        _resident((None, CONV_B_WIDTH, D_LRU), lambda i: (l, 0, 0)),
        _row(l, D_LRU), gate_w, _row(l, D_LRU), gate_w, _row(l, D_LRU), _row(l, D_LRU),
    ]
    args = (w_in_b, w_in_b, conv_w, conv_b, w_r_b, b_r, w_i_b, b_i, lam)
    if prompt:
        n = x.shape[0]
        tps = (n // seqs) // bm
        yb, nb, hl = pl.pallas_call(
            functools.partial(_a2_prompt_kernel, bm=bm, tps=tps),
            grid=(n // bm,),
            in_specs=[pl.BlockSpec((bm, D_MODEL), lambda i: (i, 0))] + weights,
            out_specs=[pl.BlockSpec((bm, D_LRU), lambda i: (i, 0)),
                       pl.BlockSpec((None, hist, D_LRU), lambda i: (i // tps, 0, 0)),
                       pl.BlockSpec((None, 1, D_LRU), lambda i: (i // tps, 0, 0))],
            out_shape=[jax.ShapeDtypeStruct((n, D_LRU), BF16),
                       jax.ShapeDtypeStruct((seqs, hist, D_LRU), F32),
                       jax.ShapeDtypeStruct((seqs, 1, D_LRU), F32)],
            scratch_shapes=[pltpu.VMEM((HIST + bm, D_LRU), F32), pltpu.VMEM((bm, D_LRU), F32),
                            pltpu.VMEM((bm, D_LRU), F32), pltpu.VMEM((bm, D_LRU), F32),
                            pltpu.VMEM((SUBLANES, D_LRU), F32)],
            compiler_params=_params(1), name=f"mixer_b_prompt_l{l}",
        )(x, *args)
        return yb, nb, hl.reshape(seqs, D_LRU)
    s_n = S_MIX_B
    return pl.pallas_call(
        functools.partial(_a2_sample_kernel, s_n=s_n),
        grid=(seqs // s_n,),
        in_specs=[pl.BlockSpec((TS, s_n, D_MODEL), lambda i: (0, i, 0)),
                  pl.BlockSpec((None, hist, s_n, D_LRU), lambda i: (l, 0, i, 0)),
                  pl.BlockSpec((None, s_n, D_LRU), lambda i: (l, i, 0))] + weights,
        out_specs=[pl.BlockSpec((TS, s_n, D_LRU), lambda i: (0, i, 0)),
                   pl.BlockSpec((hist, s_n, D_LRU), lambda i: (0, i, 0)),
                   pl.BlockSpec((s_n, D_LRU), lambda i: (i, 0))],
        out_shape=[jax.ShapeDtypeStruct((TS, seqs, D_LRU), BF16),
                   jax.ShapeDtypeStruct((hist, seqs, D_LRU), F32),
                   jax.ShapeDtypeStruct((seqs, D_LRU), F32)],
        compiler_params=_params(1), name=f"mixer_b_sample_l{l}",
    )(x, state_b, state_h, *args)


def _merge(l, x2, ya, yb, w_in_b, w_a_out_b, w_b_out_b, w_o_b, g, b, prompt):
    n = x2.shape[0]
    bm = BM_PROMPT if prompt else BM_SAMPLE
    nc = D_MODEL // CM
    return pl.pallas_call(
        functools.partial(_merge_kernel, bm=bm, nc=nc),
        grid=(n // bm, nc),
        in_specs=[
            pl.BlockSpec((bm, D_MODEL), lambda i, c: (i, 0)),
            pl.BlockSpec((bm, D_CONV), lambda i, c: (i, 0)),
            pl.BlockSpec((bm, D_LRU), lambda i, c: (i, 0)),
            pl.BlockSpec((None, D_MODEL, CM), lambda i, c: (l, 0, OFF_GA // CM + c)),
            pl.BlockSpec((None, D_MODEL, CM), lambda i, c: (l, 0, OFF_GB // CM + c)),
            pl.BlockSpec((None, D_CONV, CM), lambda i, c: (l, 0, c)),
            pl.BlockSpec((None, D_LRU, CM), lambda i, c: (l, 0, c)),
            pl.BlockSpec((None, CM, D_MODEL), lambda i, c: (l, c, 0)),
            _row(l, D_MODEL), _row(l, D_MODEL),
        ],
        out_specs=pl.BlockSpec((bm, D_MODEL), lambda i, c: (i, 0)),
        out_shape=jax.ShapeDtypeStruct((n, D_MODEL), F32),
        scratch_shapes=[pltpu.VMEM((bm, D_MODEL), BF16), pltpu.VMEM((bm, D_MODEL), F32)],
        compiler_params=_params(2), name=f"merge_{'prompt' if prompt else 'sample'}_l{l}",
    )(x2, ya, yb, w_in_b, w_in_b, w_a_out_b, w_b_out_b, w_o_b, g, b)


def _ffn_sample(l, x2, state, w_up, conv_w, conv_b, w_down, g, b, seqs):
    n = x2.shape[0]
    nc = D_FF // CF_S
    hist = FFN_CONV_WIDTH - 1
    return pl.pallas_call(
        functools.partial(_ffn_sample_kernel, s_n=seqs, nc=nc),
        grid=(nc,),
        in_specs=[
            _resident((n, D_MODEL), lambda c: (0, 0)),
            pl.BlockSpec((None, hist, seqs, CF_S), lambda c: (l, 0, 0, c)),
            pl.BlockSpec((None, D_MODEL, CF_S), lambda c: (l, 0, c)),
            pl.BlockSpec((None, D_MODEL, CF_S), lambda c: (l, 0, nc + c)),
            pl.BlockSpec((None, FFN_CONV_WIDTH, CF_S), lambda c: (l, 0, c)),
            pl.BlockSpec((None, 1, CF_S), lambda c: (l, 0, c)),
            pl.BlockSpec((None, CF_S, D_MODEL), lambda c: (l, c, 0)),
            _row(l, D_MODEL), _row(l, D_MODEL),
        ],
        out_specs=[
            pl.BlockSpec((n, D_MODEL), lambda c: (0, 0)),
            pl.BlockSpec((hist, seqs, CF_S), lambda c: (0, 0, c)),
            pl.BlockSpec((D_MODEL, CF_S), lambda c: (0, c)),
            pl.BlockSpec((D_MODEL, CF_S), lambda c: (0, c)),
            pl.BlockSpec((CF_S, D_MODEL), lambda c: (c, 0)),
        ],
        out_shape=[
            jax.ShapeDtypeStruct((n, D_MODEL), F32),
            jax.ShapeDtypeStruct((hist, seqs, D_FF), F32),
            jax.ShapeDtypeStruct((D_MODEL, D_FF), BF16),
            jax.ShapeDtypeStruct((D_MODEL, D_FF), BF16),
            jax.ShapeDtypeStruct((D_FF, D_MODEL), BF16),
        ],
        scratch_shapes=[pltpu.VMEM((n, D_MODEL), BF16)],
        compiler_params=_params(1), name=f"ffn_sample_l{l}",
    )(x2, state, w_up, w_up, conv_w, conv_b, w_down, g, b)


def _ffn_prompt(l, x2, wfu_b, wfg_b, conv_w, conv_b, wd_b, g, b, seqs):
    n = x2.shape[0]
    bm = BM_PROMPT
    nc = D_FF // CF
    hist = FFN_CONV_WIDTH - 1
    tps = (n // seqs) // bm
    x_out, nf_tiles = pl.pallas_call(
        functools.partial(_ffn_prompt_kernel, bm=bm, tps=tps, nc=nc),
        grid=(n // bm, nc),
        in_specs=[
            pl.BlockSpec((bm, D_MODEL), lambda i, c: (i, 0)),
            pl.BlockSpec((D_MODEL, CF), lambda i, c: (0, c)),
            pl.BlockSpec((D_MODEL, CF), lambda i, c: (0, c)),
            pl.BlockSpec((None, FFN_CONV_WIDTH, CF), lambda i, c: (l, 0, c)),
            pl.BlockSpec((None, 1, CF), lambda i, c: (l, 0, c)),
            pl.BlockSpec((CF, D_MODEL), lambda i, c: (c, 0)),
            _row(l, D_MODEL), _row(l, D_MODEL),
        ],
        out_specs=[pl.BlockSpec((bm, D_MODEL), lambda i, c: (i, 0)),
                   pl.BlockSpec((None, hist, CF), lambda i, c: (i, 0, c))],
        out_shape=[jax.ShapeDtypeStruct((n, D_MODEL), F32), jax.ShapeDtypeStruct((n // bm, hist, D_FF), F32)],
        scratch_shapes=[pltpu.VMEM((bm, D_MODEL), BF16), pltpu.VMEM((bm, D_MODEL), F32),
                        pltpu.VMEM((HIST + bm, CF), F32), pltpu.VMEM((nc, HIST, CF), F32)],
        compiler_params=_params(2), name=f"ffn_prompt_l{l}",
    )(x2, wfu_b, wfg_b, conv_w, conv_b, wd_b, g, b)
    return x_out, nf_tiles.reshape(seqs, tps, hist, D_FF)[:, tps - 1]


def _ple(l, x2, p2, w_pg_b, b_pg, w_pe_b, g, b, prompt):
    n = x2.shape[0]
    bm = BM_PROMPT if prompt else BM_SAMPLE
    return pl.pallas_call(
        functools.partial(_ple_kernel, bm=bm),
        grid=(n // bm,),
        in_specs=[
            pl.BlockSpec((bm, D_MODEL), lambda i: (i, 0)),
            pl.BlockSpec((None, bm, D_PLE), lambda i: (l, i, 0)),
            _resident((None, D_MODEL, D_MODEL), lambda i: (l, 0, 0)),
            _row(l, D_MODEL),
            _resident((None, D_PLE, D_MODEL), lambda i: (l, 0, 0)),
            _row(l, D_MODEL), _row(l, D_MODEL),
        ],
        out_specs=pl.BlockSpec((bm, D_MODEL), lambda i: (i, 0)),
        out_shape=jax.ShapeDtypeStruct((n, D_MODEL), F32),
        scratch_shapes=[pltpu.VMEM((bm, D_MODEL), F32)],
        compiler_params=_params(1), name=f"ple_{'prompt' if prompt else 'sample'}_l{l}",
    )(x2, p2, w_pg_b, b_pg, w_pe_b, g, b)


def kernel(x_prompt, x_sample, state_conv_a, state_conv_b, state_rglru, state_conv_ffn, p_prompt, p_sample, w_in, conv_a_w, conv_a_b, ln_a_g, ln_a_b, w_a_out, conv_b_w, conv_b_b, w_r, b_r, w_i, b_i, lru_lambda, w_b_out, w_o, ln1_g, ln1_b, w_up, ffn_conv_w, ffn_conv_b, w_down, ln2_g, ln2_b, w_pe, w_pg, b_pg, ln3_g, ln3_b):
    bp, tp, _ = x_prompt.shape
    bs, ts, _ = x_sample.shape
    assert ts == TS and tp % BM_PROMPT == 0 and tp % BM_MIXER_B == 0
    assert bs % S_MIX_A == 0 and bs % S_MIX_B == 0 and (bs * ts) % BM_SAMPLE == 0 and TS % NSPLIT_S == 0
    assert w_in.shape == (DEPTH, D_MODEL, OFF_GB + D_MODEL)

    w_in_b, w_a_out_b, w_b_out_b, w_o_b = (w.astype(BF16) for w in (w_in, w_a_out, w_b_out, w_o))
    w_pe_b, w_pg_b, w_r_b, w_i_b = (w.astype(BF16) for w in (w_pe, w_pg, w_r, w_i))
    row = lambda v: v.reshape(DEPTH, 1, v.shape[-1])
    conv_a_b, ln_a_g, ln_a_b, conv_b_b, b_r, b_i, lru_lambda = map(row, (conv_a_b, ln_a_g, ln_a_b, conv_b_b, b_r, b_i, lru_lambda))
    ln1_g, ln1_b, ffn_conv_b, ln2_g, ln2_b, b_pg, ln3_g, ln3_b = map(row, (ln1_g, ln1_b, ffn_conv_b, ln2_g, ln2_b, b_pg, ln3_g, ln3_b))

    xp = x_prompt.reshape(bp * tp, D_MODEL)
    pp = p_prompt.reshape(DEPTH, bp * tp, D_PLE)
    xs = jnp.transpose(x_sample, (1, 0, 2))
    ps = jnp.transpose(p_sample, (0, 2, 1, 3)).reshape(DEPTH, ts * bs, D_PLE)
    sa, sb, sf = (jnp.transpose(s, (0, 2, 1, 3)) for s in (state_conv_a, state_conv_b, state_conv_ffn))
    flat = lambda v: v.reshape(ts * bs, v.shape[-1])

    prompt_states, sample_states = [], []
    for l in range(DEPTH):
        ya, new_a = _mixer_a(l, xs, sa, w_in_b, conv_a_w, conv_a_b, ln_a_g, ln_a_b, False, bs)
        yb, new_b, h_last = _mixer_b(l, xs, sb, state_rglru, w_in_b, conv_b_w, conv_b_b,
                                     w_r_b, b_r, w_i_b, b_i, lru_lambda, False, bs)
        x2 = _merge(l, flat(xs), flat(ya), flat(yb), w_in_b, w_a_out_b, w_b_out_b, w_o_b, ln1_g, ln1_b, False)
        x2, new_f, wfu_b, wfg_b, wd_b = _ffn_sample(l, x2, sf, w_up, ffn_conv_w, ffn_conv_b, w_down, ln2_g, ln2_b, bs)
        xs = _ple(l, x2, ps, w_pg_b, b_pg, w_pe_b, ln3_g, ln3_b, False).reshape(ts, bs, D_MODEL)
        sample_states.append((new_a, new_b, h_last, new_f))

        ya, new_a = _mixer_a(l, xp, None, w_in_b, conv_a_w, conv_a_b, ln_a_g, ln_a_b, True, bp)
        yb, new_b, h_last = _mixer_b(l, xp, None, None, w_in_b, conv_b_w, conv_b_b,
                                     w_r_b, b_r, w_i_b, b_i, lru_lambda, True, bp)
        xp = _merge(l, xp, ya, yb, w_in_b, w_a_out_b, w_b_out_b, w_o_b, ln1_g, ln1_b, True)
        xp, new_f = _ffn_prompt(l, xp, wfu_b, wfg_b, ffn_conv_w, ffn_conv_b, wd_b, ln2_g, ln2_b, bp)
        xp = _ple(l, xp, pp, w_pg_b, b_pg, w_pe_b, ln3_g, ln3_b, True)
        prompt_states.append((new_a, new_b, h_last, new_f))

    stack = lambda states, k: jnp.stack([s[k] for s in states])
    seq_major = lambda v: jnp.transpose(v, (0, 2, 1, 3))
    return (xp.reshape(bp, tp, D_MODEL), jnp.transpose(xs, (1, 0, 2)),
            stack(prompt_states, 0), stack(prompt_states, 1), stack(prompt_states, 2), stack(prompt_states, 3),
            seq_major(stack(sample_states, 0)), seq_major(stack(sample_states, 1)), stack(sample_states, 2),
            seq_major(stack(sample_states, 3)))
```

```python
import functools
import math

import jax
import jax.numpy as jnp
from jax import lax
from jax.experimental import pallas as pl
from jax.experimental.pallas import tpu as pltpu

F32 = jnp.float32
BF16 = jnp.bfloat16

D_MODEL = 2048
DEPTH = 2
D_CONV = D_MODEL // 2
CONV_A_WIDTH = 31
D_LRU = D_MODEL
LRU_HEADS = 16
LRU_HEAD_DIM = D_LRU // LRU_HEADS
CONV_B_WIDTH = 4
LRU_C = 8.0
D_FF = 3 * D_MODEL
FFN_CONV_WIDTH = 3
D_PLE = 256
LN_EPS = 1e-5
ALPHA = (2.0 * DEPTH) ** 0.25
LOG2_E = math.log2(math.e)
OFF_AV, OFF_AG, OFF_BX, OFF_BG, OFF_GA, OFF_GB = 0, D_CONV, 2 * D_CONV, 2 * D_CONV + D_LRU, 2 * D_CONV + 2 * D_LRU, 2 * D_CONV + 2 * D_LRU + D_MODEL

SUBLANES = 8
VMEM_LIMIT_BYTES = 56 * 1024 * 1024

BM_PROMPT = 512
BM_SAMPLE = 512
BM_MIXER_B = 256
CHUNK = 512
CM = 512
CF = 1024
NSPLIT = 2
TS = 8
S_MIX_A = 32
S_MIX_B = 64
CF_S = 256
NSPLIT_S = 4
RB = 128
LN_RB = 32
LN_UNROLL = 8
LC = 128
HIST_A = 32
HIST = SUBLANES


def _dot(a, b):
    return jnp.dot(a, b, preferred_element_type=F32)


def _sigmoid(x):
    return 0.5 * jnp.tanh(0.5 * x) + 0.5


def _gelu(x):
    k = math.sqrt(2.0 / math.pi)
    hx = 0.5 * x
    return hx + hx * jnp.tanh(x * (k + (k * 0.044715) * (x * x)))


def _ln(v, g, b):
    mu = jnp.mean(v, axis=-1, keepdims=True)
    c = v - mu
    var = jnp.mean(c * c, axis=-1, keepdims=True)
    return c * lax.rsqrt(var + LN_EPS) * g + b


def _residual_ln_rows(x_ref, acc_ref, g_ref, b_ref, o_ref, bm):
    def body(r, carry):
        rows = pl.ds(pl.multiple_of(r * LN_RB, LN_RB), LN_RB)
        v = acc_ref[rows, :] if x_ref is None else ALPHA * x_ref[rows, :] + acc_ref[rows, :]
        o_ref[rows, :] = _ln(v, g_ref[...], b_ref[...])
        return carry
    lax.fori_loop(0, bm // LN_RB, body, 0, unroll=LN_UNROLL)


def _scan8(a, u):
    n, cols = a.shape
    a = a.reshape(n // SUBLANES, SUBLANES, cols)
    u = u.reshape(n // SUBLANES, SUBLANES, cols)
    rows = lax.broadcasted_iota(jnp.int32, a.shape, 1)
    for d in (1, 2, 4):
        m = rows >= d
        a_sh = pltpu.roll(a, d, 1)
        u_sh = pltpu.roll(u, d, 1)
        u = jnp.where(m, a * u_sh + u, u)
        a = jnp.where(m, a * a_sh, a)
    return a.reshape(n, cols), u.reshape(n, cols)


def _gate_dots(cb, wr_ref, wi_ref, c):
    hpc = CHUNK // LRU_HEAD_DIM
    rs, gs = [], []
    for hh in range(hpc):
        cbh = cb[:, hh * LRU_HEAD_DIM:(hh + 1) * LRU_HEAD_DIM].astype(BF16)
        rs.append(_dot(cbh, wr_ref[c * hpc + hh]))
        gs.append(_dot(cbh, wi_ref[c * hpc + hh]))
    return jnp.concatenate(rs, axis=-1), jnp.concatenate(gs, axis=-1)


def _lru_coeffs(cb, r_pre, g_pre, br_ref, bi_ref, lam_ref, c, first_tile):
    cols = slice(c * CHUNK, (c + 1) * CHUNK)
    half_k = (-0.5 * LRU_C * LOG2_E) * jax.nn.softplus(-lam_ref[:, cols])
    a = jnp.exp2(half_k * jnp.tanh(0.5 * (r_pre + br_ref[:, cols])) + half_k)
    gi = _sigmoid(g_pre + bi_ref[:, cols])
    gated = gi * cb
    u = jnp.sqrt(1.0 - a * a) * gated
    if first_tile is not None:
        head = slice(0, SUBLANES)
        row0 = lax.broadcasted_iota(jnp.int32, (SUBLANES, CHUNK), 0) == 0
        u_head = jnp.where(jnp.logical_and(row0, first_tile), gated[head], u[head])
        u = jnp.concatenate([u_head, u[SUBLANES:]], axis=0)
    return a, u


def _conv31_taps():
    taps = []
    for r in range(SUBLANES):
        for q in range(HIST_A // SUBLANES):
            j = SUBLANES * q + r
            if j < CONV_A_WIDTH:
                taps.append((r, q, CONV_A_WIDTH - 1 - j))
    return taps


def _conv31_prompt(u_scr, cw_ref, cb_ref, ca_scr, bm):
    def body(blk, carry):
        base = pl.multiple_of(blk * RB, RB)
        for lc in range(D_CONV // LC):
            lanes = pl.ds(lc * LC, LC)
            win = u_scr[pl.ds(base, RB + HIST_A), lanes]
            acc = jnp.broadcast_to(cb_ref[:, lanes], (RB, LC))
            v, v_r = win, 0
            for r, q, k in _conv31_taps():
                if r != v_r:
                    v, v_r = pltpu.roll(win, r, 0), r
                off = HIST_A - SUBLANES * q
                acc = acc + cw_ref[k:k + 1, lanes] * v[off:off + RB, :]
            ca_scr[pl.ds(base, RB), lanes] = acc
        return carry
    lax.fori_loop(0, bm // RB, body, 0)


def _ln_silu_rows(ca_scr, lg_ref, lb_ref, ya_ref, bm):
    def body(r, carry):
        rows = pl.ds(pl.multiple_of(r * RB, RB), RB)
        y = _ln(ca_scr[rows, :], lg_ref[...], lb_ref[...])
        ya_ref[rows, :] = (y * _sigmoid(y)).astype(BF16)
        return carry
    lax.fori_loop(0, bm // RB, body, 0, unroll=2)


def _a1_prompt_kernel(x_ref, wv_ref, wg_ref, cw_ref, cb_ref, lg_ref, lb_ref, ya_ref, na_ref, u_scr, ca_scr, *, bm, tps):
    t = pl.program_id(0) % tps

    @pl.when(t == 0)
    def _():
        u_scr[0:HIST_A, :] = jnp.zeros((HIST_A, D_CONV), F32)

    @pl.when(t != 0)
    def _():
        u_scr[0:HIST_A, :] = u_scr[bm:bm + HIST_A, :]

    xb = x_ref[...].astype(BF16)
    for c in range(D_CONV // CHUNK):
        cols = slice(c * CHUNK, (c + 1) * CHUNK)
        av = _dot(xb, wv_ref[:, cols])
        ag = _dot(xb, wg_ref[:, cols])
        u_scr[HIST_A:HIST_A + bm, cols] = av * _sigmoid(ag)
    na_ref[...] = u_scr[HIST_A + bm - (CONV_A_WIDTH - 1):HIST_A + bm, :]
    _conv31_prompt(u_scr, cw_ref, cb_ref, ca_scr, bm)
    _ln_silu_rows(ca_scr, lg_ref, lb_ref, ya_ref, bm)


def _a1_sample_kernel(x_ref, st_ref, wv_ref, wg_ref, cw_ref, cb_ref, lg_ref, lb_ref, ya_ref, na_ref, ext_scr, ca_scr, *, s_n):
    hist = CONV_A_WIDTH - 1
    xb = x_ref[...].reshape(TS * s_n, D_MODEL).astype(BF16)
    ext_scr[0:hist] = st_ref[...]
    for c in range(D_CONV // CHUNK):
        cols = slice(c * CHUNK, (c + 1) * CHUNK)
        av = _dot(xb, wv_ref[:, cols])
        ag = _dot(xb, wg_ref[:, cols])
        ext_scr[hist:hist + TS, :, cols] = (av * _sigmoid(ag)).reshape(TS, s_n, CHUNK)
    na_ref[...] = ext_scr[TS:TS + hist]

    def conv(t, carry):
        for lc in range(D_CONV // CHUNK):
            lanes = pl.ds(lc * CHUNK, CHUNK)
            acc = jnp.broadcast_to(cb_ref[:, lanes], (s_n, CHUNK))
            for k in range(CONV_A_WIDTH):
                acc = acc + cw_ref[k:k + 1, lanes] * ext_scr[t + k, :, lanes]
            ca_scr[t, :, lanes] = acc
        return carry
    lax.fori_loop(0, TS, conv, 0)

    def norm(t, carry):
        y = _ln(ca_scr[t], lg_ref[...], lb_ref[...])
        ya_ref[t] = (y * _sigmoid(y)).astype(BF16)
        return carry
    lax.fori_loop(0, TS, norm, 0, unroll=2)


def _a2_prompt_kernel(x_ref, wbx_ref, wbg_ref, cw_ref, cb_ref, wr_ref, br_ref, wi_ref, bi_ref, lam_ref,
                      yb_ref, nb_ref, hl_ref, bx_scr, a_scr, u_scr, g_scr, hc_scr, *, bm, tps):
    t = pl.program_id(0) % tps
    hist = CONV_B_WIDTH - 1
    nch = D_LRU // CHUNK

    @pl.when(t == 0)
    def _():
        bx_scr[0:HIST, :] = jnp.zeros((HIST, D_LRU), F32)
        hc_scr[...] = jnp.zeros((SUBLANES, D_LRU), F32)

    @pl.when(t != 0)
    def _():
        bx_scr[0:HIST, :] = bx_scr[bm:bm + HIST, :]

    xb = x_ref[...].astype(BF16)
    first_tile = t == 0
    chunk = lambda c: slice(c * CHUNK, (c + 1) * CHUNK)
    bx_next = _dot(xb, wbx_ref[:, chunk(0)])
    for c in range(nch):
        cols = chunk(c)
        bx = bx_next
        bx_scr[HIST:HIST + bm, cols] = bx
        win = bx_scr[0:HIST + bm, cols]
        cb = cb_ref[:, cols] + cw_ref[hist:hist + 1, cols] * bx
        for k in range(hist):
            cb = cb + cw_ref[k:k + 1, cols] * pltpu.roll(win, hist - k, 0)[HIST:HIST + bm, :]
        r_pre, g_pre = _gate_dots(cb, wr_ref, wi_ref, c)
        if c + 1 < nch:
            bx_next = _dot(xb, wbx_ref[:, chunk(c + 1)])
        bg = _dot(xb, wbg_ref[:, cols])
        a, u = _lru_coeffs(cb, r_pre, g_pre, br_ref, bi_ref, lam_ref, c, first_tile)
        a, u = _scan8(a, u)
        a_scr[:, cols] = a
        u_scr[:, cols] = u
        g_scr[:, cols] = _gelu(bg)
    nb_ref[...] = bx_scr[HIST + bm - hist:HIST + bm, :]

    def body(r, carry):
        rows = pl.ds(pl.multiple_of(r * 2 * SUBLANES, 2 * SUBLANES), 2 * SUBLANES)
        a2, u2 = a_scr[rows, :], u_scr[rows, :]
        h0 = u2[0:SUBLANES, :] + a2[0:SUBLANES, :] * carry
        mid = jnp.broadcast_to(h0[SUBLANES - 1:SUBLANES, :], (SUBLANES, D_LRU))
        h1 = u2[SUBLANES:, :] + a2[SUBLANES:, :] * mid
        yb_ref[rows, :] = (jnp.concatenate([h0, h1], axis=0) * g_scr[rows, :]).astype(BF16)
        return jnp.broadcast_to(h1[SUBLANES - 1:SUBLANES, :], (SUBLANES, D_LRU))
    carry = lax.fori_loop(0, bm // (2 * SUBLANES), body, hc_scr[...])
    hc_scr[...] = carry
    hl_ref[...] = carry[0:1, :]


def _a2_sample_kernel(x_ref, stb_ref, h0_ref, wbx_ref, wbg_ref, cw_ref, cb_ref, wr_ref, br_ref, wi_ref, bi_ref, lam_ref,
                      yb_ref, nb_ref, hl_ref, *, s_n):
    hist = CONV_B_WIDTH - 1
    xb = x_ref[...].reshape(TS * s_n, D_MODEL).astype(BF16)
    nch = D_LRU // CHUNK
    chunk = lambda c: slice(c * CHUNK, (c + 1) * CHUNK)
    slab = lambda v, t: v[t * s_n:(t + 1) * s_n, :]
    bx_next = _dot(xb, wbx_ref[:, chunk(0)])
    for c in range(nch):
        cols = chunk(c)
        bx = bx_next
        ext = [stb_ref[k, :, cols] for k in range(hist)] + [slab(bx, t) for t in range(TS)]
        cb = jnp.concatenate(
            [cb_ref[:, cols] + sum(cw_ref[k:k + 1, cols] * ext[t + k] for k in range(CONV_B_WIDTH)) for t in range(TS)],
            axis=0)
        r_pre, g_pre = _gate_dots(cb, wr_ref, wi_ref, c)
        if c + 1 < nch:
            bx_next = _dot(xb, wbx_ref[:, chunk(c + 1)])
        bg = _dot(xb, wbg_ref[:, cols])
        a, u = _lru_coeffs(cb, r_pre, g_pre, br_ref, bi_ref, lam_ref, c, None)
        h = h0_ref[:, cols]
        hs = []
        for t in range(TS):
            h = slab(a, t) * h + slab(u, t)
            hs.append(h)
        hl_ref[:, cols] = h
        yb_ref[:, :, cols] = (jnp.concatenate(hs, axis=0) * _gelu(bg)).astype(BF16).reshape(TS, s_n, CHUNK)
        for k in range(hist):
            nb_ref[k, :, cols] = slab(bx, TS - hist + k)


def _merge_kernel(x_ref, ya_ref, yb_ref, wga_ref, wgb_ref, wa_ref, wb_ref, wo_ref, g_ref, b_ref, o_ref, xb_scr, acc_scr, *, bm, nc):
    c = pl.program_id(1)

    @pl.when(c == 0)
    def _():
        xb_scr[...] = x_ref[...].astype(BF16)
        acc_scr[...] = jnp.zeros((bm, D_MODEL), F32)

    sb = bm // NSPLIT

    def up(s):
        rows = slice(s * sb, (s + 1) * sb)
        xb = xb_scr[rows, :]
        return (_dot(xb, wga_ref[...]), _dot(ya_ref[rows, :], wa_ref[...]),
                _dot(xb, wgb_ref[...]), _dot(yb_ref[rows, :], wb_ref[...]))

    def down(s, last, ga, out_a, gb, out_b):
        rows = slice(s * sb, (s + 1) * sb)
        merged = _sigmoid(ga) * out_a + _sigmoid(gb) * out_b
        d = _dot(merged.astype(BF16), wo_ref[...])
        if last:
            o_ref[rows, :] = _ln(ALPHA * x_ref[rows, :] + (acc_scr[rows, :] + d), g_ref[...], b_ref[...])
        else:
            acc_scr[rows, :] += d

    def body(last):
        pending = up(0)
        for s in range(NSPLIT):
            nxt = up(s + 1) if s + 1 < NSPLIT else None
            down(s, last, *pending)
            pending = nxt

    @pl.when(c != nc - 1)
    def _():
        body(False)

    @pl.when(c == nc - 1)
    def _():
        body(True)


def _ffn_prompt_kernel(x_ref, wfu_ref, wfg_ref, cw_ref, cb_ref, wd_ref, g_ref, b_ref, o_ref, nf_ref,
                       xb_scr, acc_scr, fg_scr, carry_scr, *, bm, tps, nc):
    t = pl.program_id(0) % tps
    c = pl.program_id(1)
    hist = FFN_CONV_WIDTH - 1

    @pl.when(c == 0)
    def _():
        xb_scr[...] = x_ref[...].astype(BF16)
        acc_scr[...] = jnp.zeros((bm, D_MODEL), F32)

    @pl.when(t == 0)
    def _():
        fg_scr[0:HIST, :] = jnp.zeros((HIST, CF), F32)

    @pl.when(t != 0)
    def _():
        fg_scr[0:HIST, :] = carry_scr[c]

    sb = bm // NSPLIT

    def up(s):
        xb = xb_scr[s * sb:(s + 1) * sb, :]
        fg = _dot(xb, wfg_ref[...])
        fg_scr[HIST + s * sb:HIST + (s + 1) * sb, :] = fg
        return _dot(xb, wfu_ref[...]), fg

    def down(s, last, fu, fg):
        r0 = s * sb
        fgc = cb_ref[...] + cw_ref[hist:hist + 1, :] * fg
        for k in range(hist):
            fgc = fgc + cw_ref[k:k + 1, :] * fg_scr[HIST - hist + k + r0:HIST - hist + k + r0 + sb, :]
        hf = _gelu(fgc) * fu
        d = _dot(hf.astype(BF16), wd_ref[...])
        if last:
            o_ref[r0:r0 + sb, :] = _ln(ALPHA * x_ref[r0:r0 + sb, :] + (acc_scr[r0:r0 + sb, :] + d), g_ref[...], b_ref[...])
        else:
            acc_scr[r0:r0 + sb, :] += d

    def body(last):
        pending = up(0)
        for s in range(NSPLIT):
            nxt = up(s + 1) if s + 1 < NSPLIT else None
            down(s, last, *pending)
            pending = nxt

    @pl.when(c != nc - 1)
    def _():
        body(False)

    @pl.when(c == nc - 1)
    def _():
        body(True)

    carry_scr[c] = fg_scr[bm:bm + HIST, :]
    nf_ref[...] = fg_scr[HIST + bm - hist:HIST + bm, :]


def _ffn_sample_kernel(x_ref, st_ref, wfu_ref, wfg_ref, cw_ref, cb_ref, wd_ref, g_ref, b_ref,
                       o_ref, nf_ref, wfub_ref, wfgb_ref, wdb_ref, xb_scr, *, s_n, nc):
    c = pl.program_id(0)
    hist = FFN_CONV_WIDTH - 1
    rows = TS * s_n

    @pl.when(c == 0)
    def _():
        xb_scr[...] = x_ref[...].astype(BF16)
        o_ref[...] = ALPHA * x_ref[...]

    wfub_ref[...] = wfu_ref[...].astype(BF16)
    wfgb_ref[...] = wfg_ref[...].astype(BF16)
    wdb_ref[...] = wd_ref[...].astype(BF16)

    spt = TS // NSPLIT_S
    sb = spt * s_n

    def up(j):
        xb = xb_scr[j * sb:(j + 1) * sb, :]
        return _dot(xb, wfub_ref[...]), _dot(xb, wfgb_ref[...])

    gate = [st_ref[k] for k in range(hist)]
    pending = up(0)
    for j in range(NSPLIT_S):
        nxt = up(j + 1) if j + 1 < NSPLIT_S else None
        fu, fg = pending
        gate += [fg[q * s_n:(q + 1) * s_n, :] for q in range(spt)]
        fgc = jnp.concatenate(
            [cb_ref[...] + sum(cw_ref[k:k + 1, :] * gate[j * spt + q + k] for k in range(FFN_CONV_WIDTH))
             for q in range(spt)], axis=0)
        hf = _gelu(fgc) * fu
        o_ref[j * sb:(j + 1) * sb, :] += _dot(hf.astype(BF16), wdb_ref[...])
        pending = nxt
    for k in range(hist):
        nf_ref[k] = gate[TS + k]

    @pl.when(c == nc - 1)
    def _():
        _residual_ln_rows(None, o_ref, g_ref, b_ref, o_ref, rows)


def _ple_kernel(x_ref, p_ref, wpg_ref, bpg_ref, wpe_ref, g_ref, b_ref, o_ref, e_scr, *, bm):
    xb = x_ref[...].astype(BF16)
    pb = p_ref[...].astype(BF16)
    for c in range(D_MODEL // CHUNK):
        cols = slice(c * CHUNK, (c + 1) * CHUNK)
        gate = _sigmoid(_dot(xb, wpg_ref[:, cols]) + bpg_ref[:, cols])
        e_scr[:, cols] = gate * _dot(pb, wpe_ref[:, cols])
    _residual_ln_rows(x_ref, e_scr, g_ref, b_ref, o_ref, bm)


def _params(n_axes):
    return pltpu.CompilerParams(dimension_semantics=("arbitrary",) * n_axes, vmem_limit_bytes=VMEM_LIMIT_BYTES)


def _resident(shape, index_map):
    return pl.BlockSpec(shape, index_map, pipeline_mode=pl.Buffered(1))


def _row(l, width):
    return _resident((None, 1, width), lambda *idx: (l, 0, 0))


def _mixer_a(l, x, state, w_in_b, conv_w, conv_b, ln_g, ln_b, prompt, seqs):
    hist = CONV_A_WIDTH - 1
    weights = [
        _resident((None, D_MODEL, D_CONV), lambda i: (l, 0, OFF_AV // D_CONV)),
        _resident((None, D_MODEL, D_CONV), lambda i: (l, 0, OFF_AG // D_CONV)),
        _resident((None, CONV_A_WIDTH, D_CONV), lambda i: (l, 0, 0)),
        _row(l, D_CONV), _row(l, D_CONV), _row(l, D_CONV),
    ]
    args = (w_in_b, w_in_b, conv_w, conv_b, ln_g, ln_b)
    if prompt:
        n, bm = x.shape[0], BM_PROMPT
        tps = (n // seqs) // bm
        return pl.pallas_call(
            functools.partial(_a1_prompt_kernel, bm=bm, tps=tps),
            grid=(n // bm,),
            in_specs=[pl.BlockSpec((bm, D_MODEL), lambda i: (i, 0))] + weights,
            out_specs=[pl.BlockSpec((bm, D_CONV), lambda i: (i, 0)),
                       pl.BlockSpec((None, hist, D_CONV), lambda i: (i // tps, 0, 0))],
            out_shape=[jax.ShapeDtypeStruct((n, D_CONV), BF16), jax.ShapeDtypeStruct((seqs, hist, D_CONV), F32)],
            scratch_shapes=[pltpu.VMEM((HIST_A + bm, D_CONV), F32), pltpu.VMEM((bm, D_CONV), F32)],
            compiler_params=_params(1), name=f"mixer_a_prompt_l{l}",
        )(x, *args)
    s_n = S_MIX_A
    return pl.pallas_call(
        functools.partial(_a1_sample_kernel, s_n=s_n),
        grid=(seqs // s_n,),
        in_specs=[pl.BlockSpec((TS, s_n, D_MODEL), lambda i: (0, i, 0)),
                  pl.BlockSpec((None, hist, s_n, D_CONV), lambda i: (l, 0, i, 0))] + weights,
        out_specs=[pl.BlockSpec((TS, s_n, D_CONV), lambda i: (0, i, 0)),
                   pl.BlockSpec((hist, s_n, D_CONV), lambda i: (0, i, 0))],
        out_shape=[jax.ShapeDtypeStruct((TS, seqs, D_CONV), BF16), jax.ShapeDtypeStruct((hist, seqs, D_CONV), F32)],
        scratch_shapes=[pltpu.VMEM((hist + TS, s_n, D_CONV), F32), pltpu.VMEM((TS, s_n, D_CONV), F32)],
        compiler_params=_params(1), name=f"mixer_a_sample_l{l}",
    )(x, state, *args)


def _mixer_b(l, x, state_b, state_h, w_in_b, conv_w, conv_b, w_r_b, b_r, w_i_b, b_i, lam, prompt, seqs):
    bm = BM_MIXER_B
    hist = CONV_B_WIDTH - 1
    gate_w = _resident((None, LRU_HEADS, LRU_HEAD_DIM, LRU_HEAD_DIM), lambda i: (l, 0, 0, 0))
    weights = [
        _resident((None, D_MODEL, D_LRU), lambda i: (l, 0, OFF_BX // D_LRU)),
        _resident((None, D_MODEL, D_LRU), lambda i: (l, 0, OFF_BG // D_LRU)),
        _resident((None, CONV_B_WIDTH, D_LRU), lambda i: (l, 0, 0)),
        _row(l, D_LRU), gate_w, _row(l, D_LRU), gate_w, _row(l, D_LRU), _row(l, D_LRU),
    ]
    args = (w_in_b, w_in_b, conv_w, conv_b, w_r_b, b_r, w_i_b, b_i, lam)
    if prompt:
        n = x.shape[0]
        tps = (n // seqs) // bm
        yb, nb, hl = pl.pallas_call(
            functools.partial(_a2_prompt_kernel, bm=bm, tps=tps),
            grid=(n // bm,),
            in_specs=[pl.BlockSpec((bm, D_MODEL), lambda i: (i, 0))] + weights,
            out_specs=[pl.BlockSpec((bm, D_LRU), lambda i: (i, 0)),
                       pl.BlockSpec((None, hist, D_LRU), lambda i: (i // tps, 0, 0)),
                       pl.BlockSpec((None, 1, D_LRU), lambda i: (i // tps, 0, 0))],
            out_shape=[jax.ShapeDtypeStruct((n, D_LRU), BF16),
                       jax.ShapeDtypeStruct((seqs, hist, D_LRU), F32),
                       jax.ShapeDtypeStruct((seqs, 1, D_LRU), F32)],
            scratch_shapes=[pltpu.VMEM((HIST + bm, D_LRU), F32), pltpu.VMEM((bm, D_LRU), F32),
                            pltpu.VMEM((bm, D_LRU), F32), pltpu.VMEM((bm, D_LRU), F32),
                            pltpu.VMEM((SUBLANES, D_LRU), F32)],
            compiler_params=_params(1), name=f"mixer_b_prompt_l{l}",
        )(x, *args)
        return yb, nb, hl.reshape(seqs, D_LRU)
    s_n = S_MIX_B
    return pl.pallas_call(
        functools.partial(_a2_sample_kernel, s_n=s_n),
        grid=(seqs // s_n,),
        in_specs=[pl.BlockSpec((TS, s_n, D_MODEL), lambda i: (0, i, 0)),
                  pl.BlockSpec((None, hist, s_n, D_LRU), lambda i: (l, 0, i, 0)),
                  pl.BlockSpec((None, s_n, D_LRU), lambda i: (l, i, 0))] + weights,
        out_specs=[pl.BlockSpec((TS, s_n, D_LRU), lambda i: (0, i, 0)),
                   pl.BlockSpec((hist, s_n, D_LRU), lambda i: (0, i, 0)),
                   pl.BlockSpec((s_n, D_LRU), lambda i: (i, 0))],
        out_shape=[jax.ShapeDtypeStruct((TS, seqs, D_LRU), BF16),
                   jax.ShapeDtypeStruct((hist, seqs, D_LRU), F32),
                   jax.ShapeDtypeStruct((seqs, D_LRU), F32)],
        compiler_params=_params(1), name=f"mixer_b_sample_l{l}",
    )(x, state_b, state_h, *args)


def _merge(l, x2, ya, yb, w_in_b, w_a_out_b, w_b_out_b, w_o_b, g, b, prompt):
    n = x2.shape[0]
    bm = BM_PROMPT if prompt else BM_SAMPLE
    nc = D_MODEL // CM
    return pl.pallas_call(
        functools.partial(_merge_kernel, bm=bm, nc=nc),
        grid=(n // bm, nc),
        in_specs=[
            pl.BlockSpec((bm, D_MODEL), lambda i, c: (i, 0)),
            pl.BlockSpec((bm, D_CONV), lambda i, c: (i, 0)),
            pl.BlockSpec((bm, D_LRU), lambda i, c: (i, 0)),
            pl.BlockSpec((None, D_MODEL, CM), lambda i, c: (l, 0, OFF_GA // CM + c)),
            pl.BlockSpec((None, D_MODEL, CM), lambda i, c: (l, 0, OFF_GB // CM + c)),
            pl.BlockSpec((None, D_CONV, CM), lambda i, c: (l, 0, c)),
            pl.BlockSpec((None, D_LRU, CM), lambda i, c: (l, 0, c)),
            pl.BlockSpec((None, CM, D_MODEL), lambda i, c: (l, c, 0)),
            _row(l, D_MODEL), _row(l, D_MODEL),
        ],
        out_specs=pl.BlockSpec((bm, D_MODEL), lambda i, c: (i, 0)),
        out_shape=jax.ShapeDtypeStruct((n, D_MODEL), F32),
        scratch_shapes=[pltpu.VMEM((bm, D_MODEL), BF16), pltpu.VMEM((bm, D_MODEL), F32)],
        compiler_params=_params(2), name=f"merge_{'prompt' if prompt else 'sample'}_l{l}",
    )(x2, ya, yb, w_in_b, w_in_b, w_a_out_b, w_b_out_b, w_o_b, g, b)


def _ffn_sample(l, x2, state, w_up, conv_w, conv_b, w_down, g, b, seqs):
    n = x2.shape[0]
    nc = D_FF // CF_S
    hist = FFN_CONV_WIDTH - 1
    return pl.pallas_call(
        functools.partial(_ffn_sample_kernel, s_n=seqs, nc=nc),
        grid=(nc,),
        in_specs=[
            _resident((n, D_MODEL), lambda c: (0, 0)),
            pl.BlockSpec((None, hist, seqs, CF_S), lambda c: (l, 0, 0, c)),
            pl.BlockSpec((None, D_MODEL, CF_S), lambda c: (l, 0, c)),
            pl.BlockSpec((None, D_MODEL, CF_S), lambda c: (l, 0, nc + c)),
            pl.BlockSpec((None, FFN_CONV_WIDTH, CF_S), lambda c: (l, 0, c)),
            pl.BlockSpec((None, 1, CF_S), lambda c: (l, 0, c)),
            pl.BlockSpec((None, CF_S, D_MODEL), lambda c: (l, c, 0)),
            _row(l, D_MODEL), _row(l, D_MODEL),
        ],
        out_specs=[
            pl.BlockSpec((n, D_MODEL), lambda c: (0, 0)),
            pl.BlockSpec((hist, seqs, CF_S), lambda c: (0, 0, c)),
            pl.BlockSpec((D_MODEL, CF_S), lambda c: (0, c)),
            pl.BlockSpec((D_MODEL, CF_S), lambda c: (0, c)),
            pl.BlockSpec((CF_S, D_MODEL), lambda c: (c, 0)),
        ],
        out_shape=[
            jax.ShapeDtypeStruct((n, D_MODEL), F32),
            jax.ShapeDtypeStruct((hist, seqs, D_FF), F32),
            jax.ShapeDtypeStruct((D_MODEL, D_FF), BF16),
            jax.ShapeDtypeStruct((D_MODEL, D_FF), BF16),
            jax.ShapeDtypeStruct((D_FF, D_MODEL), BF16),
        ],
        scratch_shapes=[pltpu.VMEM((n, D_MODEL), BF16)],
        compiler_params=_params(1), name=f"ffn_sample_l{l}",
    )(x2, state, w_up, w_up, conv_w, conv_b, w_down, g, b)


def _ffn_prompt(l, x2, wfu_b, wfg_b, conv_w, conv_b, wd_b, g, b, seqs):
    n = x2.shape[0]
    bm = BM_PROMPT
    nc = D_FF // CF
    hist = FFN_CONV_WIDTH - 1
    tps = (n // seqs) // bm
    x_out, nf_tiles = pl.pallas_call(
        functools.partial(_ffn_prompt_kernel, bm=bm, tps=tps, nc=nc),
        grid=(n // bm, nc),
        in_specs=[
            pl.BlockSpec((bm, D_MODEL), lambda i, c: (i, 0)),
            pl.BlockSpec((D_MODEL, CF), lambda i, c: (0, c)),
            pl.BlockSpec((D_MODEL, CF), lambda i, c: (0, c)),
            pl.BlockSpec((None, FFN_CONV_WIDTH, CF), lambda i, c: (l, 0, c)),
            pl.BlockSpec((None, 1, CF), lambda i, c: (l, 0, c)),
            pl.BlockSpec((CF, D_MODEL), lambda i, c: (c, 0)),
            _row(l, D_MODEL), _row(l, D_MODEL),
        ],
        out_specs=[pl.BlockSpec((bm, D_MODEL), lambda i, c: (i, 0)),
                   pl.BlockSpec((None, hist, CF), lambda i, c: (i, 0, c))],
        out_shape=[jax.ShapeDtypeStruct((n, D_MODEL), F32), jax.ShapeDtypeStruct((n // bm, hist, D_FF), F32)],
        scratch_shapes=[pltpu.VMEM((bm, D_MODEL), BF16), pltpu.VMEM((bm, D_MODEL), F32),
                        pltpu.VMEM((HIST + bm, CF), F32), pltpu.VMEM((nc, HIST, CF), F32)],
        compiler_params=_params(2), name=f"ffn_prompt_l{l}",
    )(x2, wfu_b, wfg_b, conv_w, conv_b, wd_b, g, b)
    return x_out, nf_tiles.reshape(seqs, tps, hist, D_FF)[:, tps - 1]


def _ple(l, x2, p2, w_pg_b, b_pg, w_pe_b, g, b, prompt):
    n = x2.shape[0]
    bm = BM_PROMPT if prompt else BM_SAMPLE
    return pl.pallas_call(
        functools.partial(_ple_kernel, bm=bm),
        grid=(n // bm,),
        in_specs=[
            pl.BlockSpec((bm, D_MODEL), lambda i: (i, 0)),
            pl.BlockSpec((None, bm, D_PLE), lambda i: (l, i, 0)),
            _resident((None, D_MODEL, D_MODEL), lambda i: (l, 0, 0)),
            _row(l, D_MODEL),
            _resident((None, D_PLE, D_MODEL), lambda i: (l, 0, 0)),
            _row(l, D_MODEL), _row(l, D_MODEL),
        ],
        out_specs=pl.BlockSpec((bm, D_MODEL), lambda i: (i, 0)),
        out_shape=jax.ShapeDtypeStruct((n, D_MODEL), F32),
        scratch_shapes=[pltpu.VMEM((bm, D_MODEL), F32)],
        compiler_params=_params(1), name=f"ple_{'prompt' if prompt else 'sample'}_l{l}",
    )(x2, p2, w_pg_b, b_pg, w_pe_b, g, b)


def kernel(x_prompt, x_sample, state_conv_a, state_conv_b, state_rglru, state_conv_ffn, p_prompt, p_sample, w_in, conv_a_w, conv_a_b, ln_a_g, ln_a_b, w_a_out, conv_b_w, conv_b_b, w_r, b_r, w_i, b_i, lru_lambda, w_b_out, w_o, ln1_g, ln1_b, w_up, ffn_conv_w, ffn_conv_b, w_down, ln2_g, ln2_b, w_pe, w_pg, b_pg, ln3_g, ln3_b):
    bp, tp, _ = x_prompt.shape
    bs, ts, _ = x_sample.shape
    assert ts == TS and tp % BM_PROMPT == 0 and tp % BM_MIXER_B == 0
    assert bs % S_MIX_A == 0 and bs % S_MIX_B == 0 and (bs * ts) % BM_SAMPLE == 0 and TS % NSPLIT_S == 0
    assert w_in.shape == (DEPTH, D_MODEL, OFF_GB + D_MODEL)

    w_in_b, w_a_out_b, w_b_out_b, w_o_b = (w.astype(BF16) for w in (w_in, w_a_out, w_b_out, w_o))
    w_pe_b, w_pg_b, w_r_b, w_i_b = (w.astype(BF16) for w in (w_pe, w_pg, w_r, w_i))
    row = lambda v: v.reshape(DEPTH, 1, v.shape[-1])
    conv_a_b, ln_a_g, ln_a_b, conv_b_b, b_r, b_i, lru_lambda = map(row, (conv_a_b, ln_a_g, ln_a_b, conv_b_b, b_r, b_i, lru_lambda))
    ln1_g, ln1_b, ffn_conv_b, ln2_g, ln2_b, b_pg, ln3_g, ln3_b = map(row, (ln1_g, ln1_b, ffn_conv_b, ln2_g, ln2_b, b_pg, ln3_g, ln3_b))

    xp = x_prompt.reshape(bp * tp, D_MODEL)
    pp = p_prompt.reshape(DEPTH, bp * tp, D_PLE)
    xs = jnp.transpose(x_sample, (1, 0, 2))
    ps = jnp.transpose(p_sample, (0, 2, 1, 3)).reshape(DEPTH, ts * bs, D_PLE)
    sa, sb, sf = (jnp.transpose(s, (0, 2, 1, 3)) for s in (state_conv_a, state_conv_b, state_conv_ffn))
    flat = lambda v: v.reshape(ts * bs, v.shape[-1])

    prompt_states, sample_states = [], []
    for l in range(DEPTH):
        ya, new_a = _mixer_a(l, xs, sa, w_in_b, conv_a_w, conv_a_b, ln_a_g, ln_a_b, False, bs)
        yb, new_b, h_last = _mixer_b(l, xs, sb, state_rglru, w_in_b, conv_b_w, conv_b_b,
                                     w_r_b, b_r, w_i_b, b_i, lru_lambda, False, bs)
        x2 = _merge(l, flat(xs), flat(ya), flat(yb), w_in_b, w_a_out_b, w_b_out_b, w_o_b, ln1_g, ln1_b, False)
        x2, new_f, wfu_b, wfg_b, wd_b = _ffn_sample(l, x2, sf, w_up, ffn_conv_w, ffn_conv_b, w_down, ln2_g, ln2_b, bs)
        xs = _ple(l, x2, ps, w_pg_b, b_pg, w_pe_b, ln3_g, ln3_b, False).reshape(ts, bs, D_MODEL)
        sample_states.append((new_a, new_b, h_last, new_f))

        ya, new_a = _mixer_a(l, xp, None, w_in_b, conv_a_w, conv_a_b, ln_a_g, ln_a_b, True, bp)
        yb, new_b, h_last = _mixer_b(l, xp, None, None, w_in_b, conv_b_w, conv_b_b,
                                     w_r_b, b_r, w_i_b, b_i, lru_lambda, True, bp)
        xp = _merge(l, xp, ya, yb, w_in_b, w_a_out_b, w_b_out_b, w_o_b, ln1_g, ln1_b, True)
        xp, new_f = _ffn_prompt(l, xp, wfu_b, wfg_b, ffn_conv_w, ffn_conv_b, wd_b, ln2_g, ln2_b, bp)
        xp = _ple(l, xp, pp, w_pg_b, b_pg, w_pe_b, ln3_g, ln3_b, True)
        prompt_states.append((new_a, new_b, h_last, new_f))

    stack = lambda states, k: jnp.stack([s[k] for s in states])
    seq_major = lambda v: jnp.transpose(v, (0, 2, 1, 3))
    return (xp.reshape(bp, tp, D_MODEL), jnp.transpose(xs, (1, 0, 2)),
            stack(prompt_states, 0), stack(prompt_states, 1), stack(prompt_states, 2), stack(prompt_states, 3),
            seq_major(stack(sample_states, 0)), seq_major(stack(sample_states, 1)), stack(sample_states, 2),
            seq_major(stack(sample_states, 3)))
```

```python
import functools
import math

import jax
import jax.numpy as jnp
from jax import lax
from jax.experimental import pallas as pl
from jax.experimental.pallas import tpu as pltpu

F32 = jnp.float32
BF16 = jnp.bfloat16

D_MODEL = 2048
DEPTH = 2
D_CONV = D_MODEL // 2
CONV_A_WIDTH = 31
D_LRU = D_MODEL
LRU_HEADS = 16
LRU_HEAD_DIM = D_LRU // LRU_HEADS
CONV_B_WIDTH = 4
LRU_C = 8.0
D_FF = 3 * D_MODEL
FFN_CONV_WIDTH = 3
D_PLE = 256
LN_EPS = 1e-5
ALPHA = (2.0 * DEPTH) ** 0.25
LOG2_E = math.log2(math.e)
OFF_AV, OFF_AG, OFF_BX, OFF_BG, OFF_GA, OFF_GB = 0, D_CONV, 2 * D_CONV, 2 * D_CONV + D_LRU, 2 * D_CONV + 2 * D_LRU, 2 * D_CONV + 2 * D_LRU + D_MODEL

SUBLANES = 8
VMEM_LIMIT_BYTES = 56 * 1024 * 1024

BM_PROMPT = 512
BM_SAMPLE = 512
BM_MIXER_B = 256
CHUNK = 512
CM = 512
CF = 1024
NSPLIT = 2
TS = 8
S_MIX_A = 32
S_MIX_B = 64
CF_S = 256
NSPLIT_S = 4
RB = 128
LN_RB = 32
LN_UNROLL = 8
LC = 128
HIST_A = 32
HIST = SUBLANES


def _dot(a, b):
    return jnp.dot(a, b, preferred_element_type=F32)


def _sigmoid(x):
    return 0.5 * jnp.tanh(0.5 * x) + 0.5


def _gelu(x):
    k = math.sqrt(2.0 / math.pi)
    hx = 0.5 * x
    return hx + hx * jnp.tanh(x * (k + (k * 0.044715) * (x * x)))


def _ln(v, g, b):
    mu = jnp.mean(v, axis=-1, keepdims=True)
    c = v - mu
    var = jnp.mean(c * c, axis=-1, keepdims=True)
    return c * lax.rsqrt(var + LN_EPS) * g + b


def _residual_ln_rows(x_ref, acc_ref, g_ref, b_ref, o_ref, bm):
    def body(r, carry):
        rows = pl.ds(pl.multiple_of(r * LN_RB, LN_RB), LN_RB)
        v = acc_ref[rows, :] if x_ref is None else ALPHA * x_ref[rows, :] + acc_ref[rows, :]
        o_ref[rows, :] = _ln(v, g_ref[...], b_ref[...])
        return carry
    lax.fori_loop(0, bm // LN_RB, body, 0, unroll=LN_UNROLL)


def _scan8(a, u):
    n, cols = a.shape
    a = a.reshape(n // SUBLANES, SUBLANES, cols)
    u = u.reshape(n // SUBLANES, SUBLANES, cols)
    rows = lax.broadcasted_iota(jnp.int32, a.shape, 1)
    for d in (1, 2, 4):
        m = rows >= d
        a_sh = pltpu.roll(a, d, 1)
        u_sh = pltpu.roll(u, d, 1)
        u = jnp.where(m, a * u_sh + u, u)
        a = jnp.where(m, a * a_sh, a)
    return a.reshape(n, cols), u.reshape(n, cols)


def _gate_dots(cb, wr_ref, wi_ref, c):
    hpc = CHUNK // LRU_HEAD_DIM
    rs, gs = [], []
    for hh in range(hpc):
        cbh = cb[:, hh * LRU_HEAD_DIM:(hh + 1) * LRU_HEAD_DIM].astype(BF16)
        rs.append(_dot(cbh, wr_ref[c * hpc + hh]))
        gs.append(_dot(cbh, wi_ref[c * hpc + hh]))
    return jnp.concatenate(rs, axis=-1), jnp.concatenate(gs, axis=-1)


def _lru_coeffs(cb, r_pre, g_pre, br_ref, bi_ref, lam_ref, c, first_tile):
    cols = slice(c * CHUNK, (c + 1) * CHUNK)
    half_k = (-0.5 * LRU_C * LOG2_E) * jax.nn.softplus(-lam_ref[:, cols])
    a = jnp.exp2(half_k * jnp.tanh(0.5 * (r_pre + br_ref[:, cols])) + half_k)
    gi = _sigmoid(g_pre + bi_ref[:, cols])
    gated = gi * cb
    u = jnp.sqrt(1.0 - a * a) * gated
    if first_tile is not None:
        head = slice(0, SUBLANES)
        row0 = lax.broadcasted_iota(jnp.int32, (SUBLANES, CHUNK), 0) == 0
        u_head = jnp.where(jnp.logical_and(row0, first_tile), gated[head], u[head])
        u = jnp.concatenate([u_head, u[SUBLANES:]], axis=0)
    return a, u


def _conv31_taps():
    taps = []
    for r in range(SUBLANES):
        for q in range(HIST_A // SUBLANES):
            j = SUBLANES * q + r
            if j < CONV_A_WIDTH:
                taps.append((r, q, CONV_A_WIDTH - 1 - j))
    return taps


def _conv31_prompt(u_scr, cw_ref, cb_ref, ca_scr, bm):
    def body(blk, carry):
        base = pl.multiple_of(blk * RB, RB)
        for lc in range(D_CONV // LC):
            lanes = pl.ds(lc * LC, LC)
            win = u_scr[pl.ds(base, RB + HIST_A), lanes]
            acc = jnp.broadcast_to(cb_ref[:, lanes], (RB, LC))
            v, v_r = win, 0
            for r, q, k in _conv31_taps():
                if r != v_r:
                    v, v_r = pltpu.roll(win, r, 0), r
                off = HIST_A - SUBLANES * q
                acc = acc + cw_ref[k:k + 1, lanes] * v[off:off + RB, :]
            ca_scr[pl.ds(base, RB), lanes] = acc
        return carry
    lax.fori_loop(0, bm // RB, body, 0)


def _ln_silu_rows(ca_scr, lg_ref, lb_ref, ya_ref, bm):
    def body(r, carry):
        rows = pl.ds(pl.multiple_of(r * RB, RB), RB)
        y = _ln(ca_scr[rows, :], lg_ref[...], lb_ref[...])
        ya_ref[rows, :] = (y * _sigmoid(y)).astype(BF16)
        return carry
    lax.fori_loop(0, bm // RB, body, 0, unroll=2)


def _a1_prompt_kernel(x_ref, wv_ref, wg_ref, cw_ref, cb_ref, lg_ref, lb_ref, ya_ref, na_ref, u_scr, ca_scr, *, bm, tps):
    t = pl.program_id(0) % tps

    @pl.when(t == 0)
    def _():
        u_scr[0:HIST_A, :] = jnp.zeros((HIST_A, D_CONV), F32)

    @pl.when(t != 0)
    def _():
        u_scr[0:HIST_A, :] = u_scr[bm:bm + HIST_A, :]

    xb = x_ref[...].astype(BF16)
    for c in range(D_CONV // CHUNK):
        cols = slice(c * CHUNK, (c + 1) * CHUNK)
        av = _dot(xb, wv_ref[:, cols])
        ag = _dot(xb, wg_ref[:, cols])
        u_scr[HIST_A:HIST_A + bm, cols] = av * _sigmoid(ag)
    na_ref[...] = u_scr[HIST_A + bm - (CONV_A_WIDTH - 1):HIST_A + bm, :]
    _conv31_prompt(u_scr, cw_ref, cb_ref, ca_scr, bm)
    _ln_silu_rows(ca_scr, lg_ref, lb_ref, ya_ref, bm)


def _a1_sample_kernel(x_ref, st_ref, wv_ref, wg_ref, cw_ref, cb_ref, lg_ref, lb_ref, ya_ref, na_ref, ext_scr, ca_scr, *, s_n):
    hist = CONV_A_WIDTH - 1
    xb = x_ref[...].reshape(TS * s_n, D_MODEL).astype(BF16)
    ext_scr[0:hist] = st_ref[...]
    for c in range(D_CONV // CHUNK):
        cols = slice(c * CHUNK, (c + 1) * CHUNK)
        av = _dot(xb, wv_ref[:, cols])
        ag = _dot(xb, wg_ref[:, cols])
        ext_scr[hist:hist + TS, :, cols] = (av * _sigmoid(ag)).reshape(TS, s_n, CHUNK)
    na_ref[...] = ext_scr[TS:TS + hist]

    def conv(t, carry):
        for lc in range(D_CONV // CHUNK):
            lanes = pl.ds(lc * CHUNK, CHUNK)
            acc = jnp.broadcast_to(cb_ref[:, lanes], (s_n, CHUNK))
            for k in range(CONV_A_WIDTH):
                acc = acc + cw_ref[k:k + 1, lanes] * ext_scr[t + k, :, lanes]
            ca_scr[t, :, lanes] = acc
        return carry
    lax.fori_loop(0, TS, conv, 0)

    def norm(t, carry):
        y = _ln(ca_scr[t], lg_ref[...], lb_ref[...])
        ya_ref[t] = (y * _sigmoid(y)).astype(BF16)
        return carry
    lax.fori_loop(0, TS, norm, 0, unroll=2)


def _a2_prompt_kernel(x_ref, wbx_ref, wbg_ref, cw_ref, cb_ref, wr_ref, br_ref, wi_ref, bi_ref, lam_ref,
                      yb_ref, nb_ref, hl_ref, bx_scr, a_scr, u_scr, g_scr, hc_scr, *, bm, tps):
    t = pl.program_id(0) % tps
    hist = CONV_B_WIDTH - 1
    nch = D_LRU // CHUNK

    @pl.when(t == 0)
    def _():
        bx_scr[0:HIST, :] = jnp.zeros((HIST, D_LRU), F32)
        hc_scr[...] = jnp.zeros((SUBLANES, D_LRU), F32)

    @pl.when(t != 0)
    def _():
        bx_scr[0:HIST, :] = bx_scr[bm:bm + HIST, :]

    xb = x_ref[...].astype(BF16)
    first_tile = t == 0
    chunk = lambda c: slice(c * CHUNK, (c + 1) * CHUNK)
    bx_next = _dot(xb, wbx_ref[:, chunk(0)])
    for c in range(nch):
        cols = chunk(c)
        bx = bx_next
        bx_scr[HIST:HIST + bm, cols] = bx
        win = bx_scr[0:HIST + bm, cols]
        cb = cb_ref[:, cols] + cw_ref[hist:hist + 1, cols] * bx
        for k in range(hist):
            cb = cb + cw_ref[k:k + 1, cols] * pltpu.roll(win, hist - k, 0)[HIST:HIST + bm, :]
        r_pre, g_pre = _gate_dots(cb, wr_ref, wi_ref, c)
        if c + 1 < nch:
            bx_next = _dot(xb, wbx_ref[:, chunk(c + 1)])
        bg = _dot(xb, wbg_ref[:, cols])
        a, u = _lru_coeffs(cb, r_pre, g_pre, br_ref, bi_ref, lam_ref, c, first_tile)
        a, u = _scan8(a, u)
        a_scr[:, cols] = a
        u_scr[:, cols] = u
        g_scr[:, cols] = _gelu(bg)
    nb_ref[...] = bx_scr[HIST + bm - hist:HIST + bm, :]

    def body(r, carry):
        rows = pl.ds(pl.multiple_of(r * 2 * SUBLANES, 2 * SUBLANES), 2 * SUBLANES)
        a2, u2 = a_scr[rows, :], u_scr[rows, :]
        h0 = u2[0:SUBLANES, :] + a2[0:SUBLANES, :] * carry
        mid = jnp.broadcast_to(h0[SUBLANES - 1:SUBLANES, :], (SUBLANES, D_LRU))
        h1 = u2[SUBLANES:, :] + a2[SUBLANES:, :] * mid
        yb_ref[rows, :] = (jnp.concatenate([h0, h1], axis=0) * g_scr[rows, :]).astype(BF16)
        return jnp.broadcast_to(h1[SUBLANES - 1:SUBLANES, :], (SUBLANES, D_LRU))
    carry = lax.fori_loop(0, bm // (2 * SUBLANES), body, hc_scr[...])
    hc_scr[...] = carry
    hl_ref[...] = carry[0:1, :]


def _a2_sample_kernel(x_ref, stb_ref, h0_ref, wbx_ref, wbg_ref, cw_ref, cb_ref, wr_ref, br_ref, wi_ref, bi_ref, lam_ref,
                      yb_ref, nb_ref, hl_ref, *, s_n):
    hist = CONV_B_WIDTH - 1
    xb = x_ref[...].reshape(TS * s_n, D_MODEL).astype(BF16)
    nch = D_LRU // CHUNK
    chunk = lambda c: slice(c * CHUNK, (c + 1) * CHUNK)
    slab = lambda v, t: v[t * s_n:(t + 1) * s_n, :]
    bx_next = _dot(xb, wbx_ref[:, chunk(0)])
    for c in range(nch):
        cols = chunk(c)
        bx = bx_next
        ext = [stb_ref[k, :, cols] for k in range(hist)] + [slab(bx, t) for t in range(TS)]
        cb = jnp.concatenate(
            [cb_ref[:, cols] + sum(cw_ref[k:k + 1, cols] * ext[t + k] for k in range(CONV_B_WIDTH)) for t in range(TS)],
            axis=0)
        r_pre, g_pre = _gate_dots(cb, wr_ref, wi_ref, c)
        if c + 1 < nch:
            bx_next = _dot(xb, wbx_ref[:, chunk(c + 1)])
        bg = _dot(xb, wbg_ref[:, cols])
        a, u = _lru_coeffs(cb, r_pre, g_pre, br_ref, bi_ref, lam_ref, c, None)
        h = h0_ref[:, cols]
        hs = []
        for t in range(TS):
            h = slab(a, t) * h + slab(u, t)
            hs.append(h)
        hl_ref[:, cols] = h
        yb_ref[:, :, cols] = (jnp.concatenate(hs, axis=0) * _gelu(bg)).astype(BF16).reshape(TS, s_n, CHUNK)
        for k in range(hist):
            nb_ref[k, :, cols] = slab(bx, TS - hist + k)


def _merge_kernel(x_ref, ya_ref, yb_ref, wga_ref, wgb_ref, wa_ref, wb_ref, wo_ref, g_ref, b_ref, o_ref, acc_scr, *, bm, nc):
    c = pl.program_id(1)

    @pl.when(c == 0)
    def _():
        acc_scr[...] = jnp.zeros((bm, D_MODEL), F32)

    sb = bm // NSPLIT

    def up(s):
        rows = slice(s * sb, (s + 1) * sb)
        xb = x_ref[rows, :].astype(BF16)
        return (_dot(xb, wga_ref[...]), _dot(ya_ref[rows, :], wa_ref[...]),
                _dot(xb, wgb_ref[...]), _dot(yb_ref[rows, :], wb_ref[...]))

    def down(s, last, ga, out_a, gb, out_b):
        rows = slice(s * sb, (s + 1) * sb)
        merged = _sigmoid(ga) * out_a + _sigmoid(gb) * out_b
        d = _dot(merged.astype(BF16), wo_ref[...])
        if last:
            o_ref[rows, :] = _ln(ALPHA * x_ref[rows, :] + (acc_scr[rows, :] + d), g_ref[...], b_ref[...])
        else:
            acc_scr[rows, :] += d

    def body(last):
        pending = up(0)
        for s in range(NSPLIT):
            nxt = up(s + 1) if s + 1 < NSPLIT else None
            down(s, last, *pending)
            pending = nxt

    @pl.when(c != nc - 1)
    def _():
        body(False)

    @pl.when(c == nc - 1)
    def _():
        body(True)


def _ffn_prompt_kernel(x_ref, wfu_ref, wfg_ref, cw_ref, cb_ref, wd_ref, g_ref, b_ref, o_ref, nf_ref,
                       acc_scr, fg_scr, carry_scr, *, bm, tps, nc):
    t = pl.program_id(0) % tps
    c = pl.program_id(1)
    hist = FFN_CONV_WIDTH - 1

    @pl.when(c == 0)
    def _():
        acc_scr[...] = jnp.zeros((bm, D_MODEL), F32)

    @pl.when(t == 0)
    def _():
        fg_scr[0:HIST, :] = jnp.zeros((HIST, CF), F32)

    @pl.when(t != 0)
    def _():
        fg_scr[0:HIST, :] = carry_scr[c]

    sb = bm // NSPLIT

    def up(s):
        xb = x_ref[s * sb:(s + 1) * sb, :].astype(BF16)
        fg = _dot(xb, wfg_ref[...])
        fg_scr[HIST + s * sb:HIST + (s + 1) * sb, :] = fg
        return _dot(xb, wfu_ref[...]), fg

    def down(s, last, fu, fg):
        r0 = s * sb
        fgc = cb_ref[...] + cw_ref[hist:hist + 1, :] * fg
        for k in range(hist):
            fgc = fgc + cw_ref[k:k + 1, :] * fg_scr[HIST - hist + k + r0:HIST - hist + k + r0 + sb, :]
        hf = _gelu(fgc) * fu
        d = _dot(hf.astype(BF16), wd_ref[...])
        if last:
            o_ref[r0:r0 + sb, :] = _ln(ALPHA * x_ref[r0:r0 + sb, :] + (acc_scr[r0:r0 + sb, :] + d), g_ref[...], b_ref[...])
        else:
            acc_scr[r0:r0 + sb, :] += d

    def body(last):
        pending = up(0)
        for s in range(NSPLIT):
            nxt = up(s + 1) if s + 1 < NSPLIT else None
            down(s, last, *pending)
            pending = nxt

    @pl.when(c != nc - 1)
    def _():
        body(False)

    @pl.when(c == nc - 1)
    def _():
        body(True)

    carry_scr[c] = fg_scr[bm:bm + HIST, :]
    nf_ref[...] = fg_scr[HIST + bm - hist:HIST + bm, :]


def _ffn_sample_kernel(x_ref, st_ref, wfu_ref, wfg_ref, cw_ref, cb_ref, wd_ref, g_ref, b_ref,
                       o_ref, nf_ref, wfub_ref, wfgb_ref, wdb_ref, xb_scr, *, s_n, nc):
    c = pl.program_id(0)
    hist = FFN_CONV_WIDTH - 1
    rows = TS * s_n

    @pl.when(c == 0)
    def _():
        xb_scr[...] = x_ref[...].astype(BF16)
        o_ref[...] = ALPHA * x_ref[...]

    wfub_ref[...] = wfu_ref[...].astype(BF16)
    wfgb_ref[...] = wfg_ref[...].astype(BF16)
    wdb_ref[...] = wd_ref[...].astype(BF16)

    spt = TS // NSPLIT_S
    sb = spt * s_n

    def up(j):
        xb = xb_scr[j * sb:(j + 1) * sb, :]
        return _dot(xb, wfub_ref[...]), _dot(xb, wfgb_ref[...])

    gate = [st_ref[k] for k in range(hist)]
    pending = up(0)
    for j in range(NSPLIT_S):
        nxt = up(j + 1) if j + 1 < NSPLIT_S else None
        fu, fg = pending
        gate += [fg[q * s_n:(q + 1) * s_n, :] for q in range(spt)]
        fgc = jnp.concatenate(
            [cb_ref[...] + sum(cw_ref[k:k + 1, :] * gate[j * spt + q + k] for k in range(FFN_CONV_WIDTH))
             for q in range(spt)], axis=0)
        hf = _gelu(fgc) * fu
        o_ref[j * sb:(j + 1) * sb, :] += _dot(hf.astype(BF16), wdb_ref[...])
        pending = nxt
    for k in range(hist):
        nf_ref[k] = gate[TS + k]

    @pl.when(c == nc - 1)
    def _():
        _residual_ln_rows(None, o_ref, g_ref, b_ref, o_ref, rows)


def _ple_kernel(x_ref, p_ref, wpg_ref, bpg_ref, wpe_ref, g_ref, b_ref, o_ref, e_scr, *, bm):
    sb = bm // NSPLIT

    def embed(s):
        rows = slice(s * sb, (s + 1) * sb)
        xb = x_ref[rows, :].astype(BF16)
        pb = p_ref[rows, :].astype(BF16)
        for c in range(D_MODEL // CHUNK):
            cols = slice(c * CHUNK, (c + 1) * CHUNK)
            gate = _sigmoid(_dot(xb, wpg_ref[:, cols]) + bpg_ref[:, cols])
            e_scr[rows, cols] = gate * _dot(pb, wpe_ref[:, cols])

    embed(0)
    for s in range(NSPLIT):
        if s + 1 < NSPLIT:
            embed(s + 1)
        rows = slice(s * sb, (s + 1) * sb)
        o_ref[rows, :] = _ln(ALPHA * x_ref[rows, :] + e_scr[rows, :], g_ref[...], b_ref[...])


def _params(n_axes):
    return pltpu.CompilerParams(dimension_semantics=("arbitrary",) * n_axes, vmem_limit_bytes=VMEM_LIMIT_BYTES)


def _resident(shape, index_map):
    return pl.BlockSpec(shape, index_map, pipeline_mode=pl.Buffered(1))


def _row(l, width):
    return _resident((None, 1, width), lambda *idx: (l, 0, 0))


def _mixer_a(l, x, state, w_in_b, conv_w, conv_b, ln_g, ln_b, prompt, seqs):
    hist = CONV_A_WIDTH - 1
    weights = [
        _resident((None, D_MODEL, D_CONV), lambda i: (l, 0, OFF_AV // D_CONV)),
        _resident((None, D_MODEL, D_CONV), lambda i: (l, 0, OFF_AG // D_CONV)),
        _resident((None, CONV_A_WIDTH, D_CONV), lambda i: (l, 0, 0)),
        _row(l, D_CONV), _row(l, D_CONV), _row(l, D_CONV),
    ]
    args = (w_in_b, w_in_b, conv_w, conv_b, ln_g, ln_b)
    if prompt:
        n, bm = x.shape[0], BM_PROMPT
        tps = (n // seqs) // bm
        return pl.pallas_call(
            functools.partial(_a1_prompt_kernel, bm=bm, tps=tps),
            grid=(n // bm,),
            in_specs=[pl.BlockSpec((bm, D_MODEL), lambda i: (i, 0))] + weights,
            out_specs=[pl.BlockSpec((bm, D_CONV), lambda i: (i, 0)),
                       pl.BlockSpec((None, hist, D_CONV), lambda i: (i // tps, 0, 0))],
            out_shape=[jax.ShapeDtypeStruct((n, D_CONV), BF16), jax.ShapeDtypeStruct((seqs, hist, D_CONV), F32)],
            scratch_shapes=[pltpu.VMEM((HIST_A + bm, D_CONV), F32), pltpu.VMEM((bm, D_CONV), F32)],
            compiler_params=_params(1), name=f"mixer_a_prompt_l{l}",
        )(x, *args)
    s_n = S_MIX_A
    return pl.pallas_call(
        functools.partial(_a1_sample_kernel, s_n=s_n),
        grid=(seqs // s_n,),
        in_specs=[pl.BlockSpec((TS, s_n, D_MODEL), lambda i: (0, i, 0)),
                  pl.BlockSpec((None, hist, s_n, D_CONV), lambda i: (l, 0, i, 0))] + weights,
        out_specs=[pl.BlockSpec((TS, s_n, D_CONV), lambda i: (0, i, 0)),
                   pl.BlockSpec((hist, s_n, D_CONV), lambda i: (0, i, 0))],
        out_shape=[jax.ShapeDtypeStruct((TS, seqs, D_CONV), BF16), jax.ShapeDtypeStruct((hist, seqs, D_CONV), F32)],
        scratch_shapes=[pltpu.VMEM((hist + TS, s_n, D_CONV), F32), pltpu.VMEM((TS, s_n, D_CONV), F32)],
        compiler_params=_params(1), name=f"mixer_a_sample_l{l}",
    )(x, state, *args)


def _mixer_b(l, x, state_b, state_h, w_in_b, conv_w, conv_b, w_r_b, b_r, w_i_b, b_i, lam, prompt, seqs):
    bm = BM_MIXER_B
    hist = CONV_B_WIDTH - 1
    gate_w = _resident((None, LRU_HEADS, LRU_HEAD_DIM, LRU_HEAD_DIM), lambda i: (l, 0, 0, 0))
    weights = [
        _resident((None, D_MODEL, D_LRU), lambda i: (l, 0, OFF_BX // D_LRU)),
        _resident((None, D_MODEL, D_LRU), lambda i: (l, 0, OFF_BG // D_LRU)),
        _resident((None, CONV_B_WIDTH, D_LRU), lambda i: (l, 0, 0)),
        _row(l, D_LRU), gate_w, _row(l, D_LRU), gate_w, _row(l, D_LRU), _row(l, D_LRU),
    ]
    args = (w_in_b, w_in_b, conv_w, conv_b, w_r_b, b_r, w_i_b, b_i, lam)
    if prompt:
        n = x.shape[0]
        tps = (n // seqs) // bm
        yb, nb, hl = pl.pallas_call(
            functools.partial(_a2_prompt_kernel, bm=bm, tps=tps),
            grid=(n // bm,),
            in_specs=[pl.BlockSpec((bm, D_MODEL), lambda i: (i, 0))] + weights,
            out_specs=[pl.BlockSpec((bm, D_LRU), lambda i: (i, 0)),
                       pl.BlockSpec((None, hist, D_LRU), lambda i: (i // tps, 0, 0)),
                       pl.BlockSpec((None, 1, D_LRU), lambda i: (i // tps, 0, 0))],
            out_shape=[jax.ShapeDtypeStruct((n, D_LRU), BF16),
                       jax.ShapeDtypeStruct((seqs, hist, D_LRU), F32),
                       jax.ShapeDtypeStruct((seqs, 1, D_LRU), F32)],
            scratch_shapes=[pltpu.VMEM((HIST + bm, D_LRU), F32), pltpu.VMEM((bm, D_LRU), F32),
                            pltpu.VMEM((bm, D_LRU), F32), pltpu.VMEM((bm, D_LRU), F32),
                            pltpu.VMEM((SUBLANES, D_LRU), F32)],
            compiler_params=_params(1), name=f"mixer_b_prompt_l{l}",
        )(x, *args)
        return yb, nb, hl.reshape(seqs, D_LRU)
    s_n = S_MIX_B
    return pl.pallas_call(
        functools.partial(_a2_sample_kernel, s_n=s_n),
        grid=(seqs // s_n,),
        in_specs=[pl.BlockSpec((TS, s_n, D_MODEL), lambda i: (0, i, 0)),
                  pl.BlockSpec((None, hist, s_n, D_LRU), lambda i: (l, 0, i, 0)),
                  pl.BlockSpec((None, s_n, D_LRU), lambda i: (l, i, 0))] + weights,
        out_specs=[pl.BlockSpec((TS, s_n, D_LRU), lambda i: (0, i, 0)),
                   pl.BlockSpec((hist, s_n, D_LRU), lambda i: (0, i, 0)),
                   pl.BlockSpec((s_n, D_LRU), lambda i: (i, 0))],
        out_shape=[jax.ShapeDtypeStruct((TS, seqs, D_LRU), BF16),
                   jax.ShapeDtypeStruct((hist, seqs, D_LRU), F32),
                   jax.ShapeDtypeStruct((seqs, D_LRU), F32)],
        compiler_params=_params(1), name=f"mixer_b_sample_l{l}",
    )(x, state_b, state_h, *args)


def _merge(l, x2, ya, yb, w_in_b, w_a_out_b, w_b_out_b, w_o_b, g, b, prompt):
    n = x2.shape[0]
    bm = BM_PROMPT if prompt else BM_SAMPLE
    nc = D_MODEL // CM
    return pl.pallas_call(
        functools.partial(_merge_kernel, bm=bm, nc=nc),
        grid=(n // bm, nc),
        in_specs=[
            pl.BlockSpec((bm, D_MODEL), lambda i, c: (i, 0)),
            pl.BlockSpec((bm, D_CONV), lambda i, c: (i, 0)),
            pl.BlockSpec((bm, D_LRU), lambda i, c: (i, 0)),
            pl.BlockSpec((None, D_MODEL, CM), lambda i, c: (l, 0, OFF_GA // CM + c)),
            pl.BlockSpec((None, D_MODEL, CM), lambda i, c: (l, 0, OFF_GB // CM + c)),
            pl.BlockSpec((None, D_CONV, CM), lambda i, c: (l, 0, c)),
            pl.BlockSpec((None, D_LRU, CM), lambda i, c: (l, 0, c)),
            pl.BlockSpec((None, CM, D_MODEL), lambda i, c: (l, c, 0)),
            _row(l, D_MODEL), _row(l, D_MODEL),
        ],
        out_specs=pl.BlockSpec((bm, D_MODEL), lambda i, c: (i, 0)),
        out_shape=jax.ShapeDtypeStruct((n, D_MODEL), F32),
        scratch_shapes=[pltpu.VMEM((bm, D_MODEL), F32)],
        compiler_params=_params(2), name=f"merge_{'prompt' if prompt else 'sample'}_l{l}",
    )(x2, ya, yb, w_in_b, w_in_b, w_a_out_b, w_b_out_b, w_o_b, g, b)


def _ffn_sample(l, x2, state, w_up, conv_w, conv_b, w_down, g, b, seqs):
    n = x2.shape[0]
    nc = D_FF // CF_S
    hist = FFN_CONV_WIDTH - 1
    return pl.pallas_call(
        functools.partial(_ffn_sample_kernel, s_n=seqs, nc=nc),
        grid=(nc,),
        in_specs=[
            _resident((n, D_MODEL), lambda c: (0, 0)),
            pl.BlockSpec((None, hist, seqs, CF_S), lambda c: (l, 0, 0, c)),
            pl.BlockSpec((None, D_MODEL, CF_S), lambda c: (l, 0, c)),
            pl.BlockSpec((None, D_MODEL, CF_S), lambda c: (l, 0, nc + c)),
            pl.BlockSpec((None, FFN_CONV_WIDTH, CF_S), lambda c: (l, 0, c)),
            pl.BlockSpec((None, 1, CF_S), lambda c: (l, 0, c)),
            pl.BlockSpec((None, CF_S, D_MODEL), lambda c: (l, c, 0)),
            _row(l, D_MODEL), _row(l, D_MODEL),
        ],
        out_specs=[
            pl.BlockSpec((n, D_MODEL), lambda c: (0, 0)),
            pl.BlockSpec((hist, seqs, CF_S), lambda c: (0, 0, c)),
            pl.BlockSpec((D_MODEL, CF_S), lambda c: (0, c)),
            pl.BlockSpec((D_MODEL, CF_S), lambda c: (0, c)),
            pl.BlockSpec((CF_S, D_MODEL), lambda c: (c, 0)),
        ],
        out_shape=[
            jax.ShapeDtypeStruct((n, D_MODEL), F32),
            jax.ShapeDtypeStruct((hist, seqs, D_FF), F32),
            jax.ShapeDtypeStruct((D_MODEL, D_FF), BF16),
            jax.ShapeDtypeStruct((D_MODEL, D_FF), BF16),
            jax.ShapeDtypeStruct((D_FF, D_MODEL), BF16),
        ],
        scratch_shapes=[pltpu.VMEM((n, D_MODEL), BF16)],
        compiler_params=_params(1), name=f"ffn_sample_l{l}",
    )(x2, state, w_up, w_up, conv_w, conv_b, w_down, g, b)


def _ffn_prompt(l, x2, wfu_b, wfg_b, conv_w, conv_b, wd_b, g, b, seqs):
    n = x2.shape[0]
    bm = BM_PROMPT
    nc = D_FF // CF
    hist = FFN_CONV_WIDTH - 1
    tps = (n // seqs) // bm
    x_out, nf_tiles = pl.pallas_call(
        functools.partial(_ffn_prompt_kernel, bm=bm, tps=tps, nc=nc),
        grid=(n // bm, nc),
        in_specs=[
            pl.BlockSpec((bm, D_MODEL), lambda i, c: (i, 0)),
            pl.BlockSpec((D_MODEL, CF), lambda i, c: (0, c)),
            pl.BlockSpec((D_MODEL, CF), lambda i, c: (0, c)),
            pl.BlockSpec((None, FFN_CONV_WIDTH, CF), lambda i, c: (l, 0, c)),
            pl.BlockSpec((None, 1, CF), lambda i, c: (l, 0, c)),
            pl.BlockSpec((CF, D_MODEL), lambda i, c: (c, 0)),
            _row(l, D_MODEL), _row(l, D_MODEL),
        ],
        out_specs=[pl.BlockSpec((bm, D_MODEL), lambda i, c: (i, 0)),
                   pl.BlockSpec((None, hist, CF), lambda i, c: (i, 0, c))],
        out_shape=[jax.ShapeDtypeStruct((n, D_MODEL), F32), jax.ShapeDtypeStruct((n // bm, hist, D_FF), F32)],
        scratch_shapes=[pltpu.VMEM((bm, D_MODEL), F32),
                        pltpu.VMEM((HIST + bm, CF), F32), pltpu.VMEM((nc, HIST, CF), F32)],
        compiler_params=_params(2), name=f"ffn_prompt_l{l}",
    )(x2, wfu_b, wfg_b, conv_w, conv_b, wd_b, g, b)
    return x_out, nf_tiles.reshape(seqs, tps, hist, D_FF)[:, tps - 1]


def _ple(l, x2, p2, w_pg_b, b_pg, w_pe_b, g, b, prompt):
    n = x2.shape[0]
    bm = BM_PROMPT if prompt else BM_SAMPLE
    return pl.pallas_call(
        functools.partial(_ple_kernel, bm=bm),
        grid=(n // bm,),
        in_specs=[
            pl.BlockSpec((bm, D_MODEL), lambda i: (i, 0)),
            pl.BlockSpec((None, bm, D_PLE), lambda i: (l, i, 0)),
            _resident((None, D_MODEL, D_MODEL), lambda i: (l, 0, 0)),
            _row(l, D_MODEL),
            _resident((None, D_PLE, D_MODEL), lambda i: (l, 0, 0)),
            _row(l, D_MODEL), _row(l, D_MODEL),
        ],
        out_specs=pl.BlockSpec((bm, D_MODEL), lambda i: (i, 0)),
        out_shape=jax.ShapeDtypeStruct((n, D_MODEL), F32),
        scratch_shapes=[pltpu.VMEM((bm, D_MODEL), F32)],
        compiler_params=_params(1), name=f"ple_{'prompt' if prompt else 'sample'}_l{l}",
    )(x2, p2, w_pg_b, b_pg, w_pe_b, g, b)


def kernel(x_prompt, x_sample, state_conv_a, state_conv_b, state_rglru, state_conv_ffn, p_prompt, p_sample, w_in, conv_a_w, conv_a_b, ln_a_g, ln_a_b, w_a_out, conv_b_w, conv_b_b, w_r, b_r, w_i, b_i, lru_lambda, w_b_out, w_o, ln1_g, ln1_b, w_up, ffn_conv_w, ffn_conv_b, w_down, ln2_g, ln2_b, w_pe, w_pg, b_pg, ln3_g, ln3_b):
    bp, tp, _ = x_prompt.shape
    bs, ts, _ = x_sample.shape
    assert ts == TS and tp % BM_PROMPT == 0 and tp % BM_MIXER_B == 0
    assert bs % S_MIX_A == 0 and bs % S_MIX_B == 0 and (bs * ts) % BM_SAMPLE == 0 and TS % NSPLIT_S == 0
    assert w_in.shape == (DEPTH, D_MODEL, OFF_GB + D_MODEL)

    w_in_b, w_a_out_b, w_b_out_b, w_o_b = (w.astype(BF16) for w in (w_in, w_a_out, w_b_out, w_o))
    w_pe_b, w_pg_b, w_r_b, w_i_b = (w.astype(BF16) for w in (w_pe, w_pg, w_r, w_i))
    row = lambda v: v.reshape(DEPTH, 1, v.shape[-1])
    conv_a_b, ln_a_g, ln_a_b, conv_b_b, b_r, b_i, lru_lambda = map(row, (conv_a_b, ln_a_g, ln_a_b, conv_b_b, b_r, b_i, lru_lambda))
    ln1_g, ln1_b, ffn_conv_b, ln2_g, ln2_b, b_pg, ln3_g, ln3_b = map(row, (ln1_g, ln1_b, ffn_conv_b, ln2_g, ln2_b, b_pg, ln3_g, ln3_b))

    xp = x_prompt.reshape(bp * tp, D_MODEL)
    pp = p_prompt.reshape(DEPTH, bp * tp, D_PLE)
    xs = jnp.transpose(x_sample, (1, 0, 2))
    ps = jnp.transpose(p_sample, (0, 2, 1, 3)).reshape(DEPTH, ts * bs, D_PLE)
    sa, sb, sf = (jnp.transpose(s, (0, 2, 1, 3)) for s in (state_conv_a, state_conv_b, state_conv_ffn))
    flat = lambda v: v.reshape(ts * bs, v.shape[-1])

    prompt_states, sample_states = [], []
    for l in range(DEPTH):
        ya, new_a = _mixer_a(l, xs, sa, w_in_b, conv_a_w, conv_a_b, ln_a_g, ln_a_b, False, bs)
        yb, new_b, h_last = _mixer_b(l, xs, sb, state_rglru, w_in_b, conv_b_w, conv_b_b,
                                     w_r_b, b_r, w_i_b, b_i, lru_lambda, False, bs)
        x2 = _merge(l, flat(xs), flat(ya), flat(yb), w_in_b, w_a_out_b, w_b_out_b, w_o_b, ln1_g, ln1_b, False)
        x2, new_f, wfu_b, wfg_b, wd_b = _ffn_sample(l, x2, sf, w_up, ffn_conv_w, ffn_conv_b, w_down, ln2_g, ln2_b, bs)
        xs = _ple(l, x2, ps, w_pg_b, b_pg, w_pe_b, ln3_g, ln3_b, False).reshape(ts, bs, D_MODEL)
        sample_states.append((new_a, new_b, h_last, new_f))

        ya, new_a = _mixer_a(l, xp, None, w_in_b, conv_a_w, conv_a_b, ln_a_g, ln_a_b, True, bp)
        yb, new_b, h_last = _mixer_b(l, xp, None, None, w_in_b, conv_b_w, conv_b_b,
                                     w_r_b, b_r, w_i_b, b_i, lru_lambda, True, bp)
        xp = _merge(l, xp, ya, yb, w_in_b, w_a_out_b, w_b_out_b, w_o_b, ln1_g, ln1_b, True)
        xp, new_f = _ffn_prompt(l, xp, wfu_b, wfg_b, ffn_conv_w, ffn_conv_b, wd_b, ln2_g, ln2_b, bp)
        xp = _ple(l, xp, pp, w_pg_b, b_pg, w_pe_b, ln3_g, ln3_b, True)
        prompt_states.append((new_a, new_b, h_last, new_f))

    stack = lambda states, k: jnp.stack([s[k] for s in states])
    seq_major = lambda v: jnp.transpose(v, (0, 2, 1, 3))
    return (xp.reshape(bp, tp, D_MODEL), jnp.transpose(xs, (1, 0, 2)),
            stack(prompt_states, 0), stack(prompt_states, 1), stack(prompt_states, 2), stack(prompt_states, 3),
            seq_major(stack(sample_states, 0)), seq_major(stack(sample_states, 1)), stack(sample_states, 2),
            seq_major(stack(sample_states, 3)))
```

```python
import functools
import math

import jax
import jax.numpy as jnp
from jax import lax
from jax.experimental import pallas as pl
from jax.experimental.pallas import tpu as pltpu

F32 = jnp.float32
BF16 = jnp.bfloat16

D_MODEL = 2048
DEPTH = 2
D_CONV = D_MODEL // 2
CONV_A_WIDTH = 31
D_LRU = D_MODEL
LRU_HEADS = 16
LRU_HEAD_DIM = D_LRU // LRU_HEADS
CONV_B_WIDTH = 4
LRU_C = 8.0
D_FF = 3 * D_MODEL
FFN_CONV_WIDTH = 3
D_PLE = 256
LN_EPS = 1e-5
ALPHA = (2.0 * DEPTH) ** 0.25
LOG2_E = math.log2(math.e)
OFF_AV, OFF_AG, OFF_BX, OFF_BG, OFF_GA, OFF_GB = 0, D_CONV, 2 * D_CONV, 2 * D_CONV + D_LRU, 2 * D_CONV + 2 * D_LRU, 2 * D_CONV + 2 * D_LRU + D_MODEL

SUBLANES = 8
VMEM_LIMIT_BYTES = 56 * 1024 * 1024

BM_PROMPT = 512
BM_SAMPLE = 512
BM_MIXER_B = 256
CHUNK = 512
CM = 512
CF = 1024
NSPLIT = 2
TS = 8
S_MIX_A = 32
S_MIX_B = 64
CF_S = 256
NSPLIT_S = 4
RB = 128
LN_RB = 32
LN_UNROLL = 8
LC = 128
HIST_A = 32
HIST = SUBLANES


def _dot(a, b):
    return jnp.dot(a, b, preferred_element_type=F32)


def _sigmoid(x):
    return 0.5 * jnp.tanh(0.5 * x) + 0.5


def _gelu(x):
    k = math.sqrt(2.0 / math.pi)
    hx = 0.5 * x
    return hx + hx * jnp.tanh(x * (k + (k * 0.044715) * (x * x)))


def _ln(v, g, b):
    mu = jnp.mean(v, axis=-1, keepdims=True)
    c = v - mu
    var = jnp.mean(c * c, axis=-1, keepdims=True)
    return c * lax.rsqrt(var + LN_EPS) * g + b


def _residual_ln_rows(x_ref, acc_ref, g_ref, b_ref, o_ref, bm):
    def body(r, carry):
        rows = pl.ds(pl.multiple_of(r * LN_RB, LN_RB), LN_RB)
        v = acc_ref[rows, :] if x_ref is None else ALPHA * x_ref[rows, :] + acc_ref[rows, :]
        o_ref[rows, :] = _ln(v, g_ref[...], b_ref[...])
        return carry
    lax.fori_loop(0, bm // LN_RB, body, 0, unroll=LN_UNROLL)


def _scan8(a, u):
    n, cols = a.shape
    a = a.reshape(n // SUBLANES, SUBLANES, cols)
    u = u.reshape(n // SUBLANES, SUBLANES, cols)
    rows = lax.broadcasted_iota(jnp.int32, a.shape, 1)
    for d in (1, 2, 4):
        m = rows >= d
        a_sh = pltpu.roll(a, d, 1)
        u_sh = pltpu.roll(u, d, 1)
        u = jnp.where(m, a * u_sh + u, u)
        a = jnp.where(m, a * a_sh, a)
    return a.reshape(n, cols), u.reshape(n, cols)


def _gate_dots(cb, wr_ref, wi_ref, c):
    hpc = CHUNK // LRU_HEAD_DIM
    rs, gs = [], []
    for hh in range(hpc):
        cbh = cb[:, hh * LRU_HEAD_DIM:(hh + 1) * LRU_HEAD_DIM].astype(BF16)
        rs.append(_dot(cbh, wr_ref[c * hpc + hh]))
        gs.append(_dot(cbh, wi_ref[c * hpc + hh]))
    return jnp.concatenate(rs, axis=-1), jnp.concatenate(gs, axis=-1)


def _lru_coeffs(cb, r_pre, g_pre, br_ref, bi_ref, lam_ref, c, first_tile):
    cols = slice(c * CHUNK, (c + 1) * CHUNK)
    half_k = (-0.5 * LRU_C * LOG2_E) * jax.nn.softplus(-lam_ref[:, cols])
    a = jnp.exp2(half_k * jnp.tanh(0.5 * (r_pre + br_ref[:, cols])) + half_k)
    gi = _sigmoid(g_pre + bi_ref[:, cols])
    gated = gi * cb
    u = jnp.sqrt(1.0 - a * a) * gated
    if first_tile is not None:
        head = slice(0, SUBLANES)
        row0 = lax.broadcasted_iota(jnp.int32, (SUBLANES, CHUNK), 0) == 0
        u_head = jnp.where(jnp.logical_and(row0, first_tile), gated[head], u[head])
        u = jnp.concatenate([u_head, u[SUBLANES:]], axis=0)
    return a, u


def _conv31_taps():
    taps = []
    for r in range(SUBLANES):
        for q in range(HIST_A // SUBLANES):
            j = SUBLANES * q + r
            if j < CONV_A_WIDTH:
                taps.append((r, q, CONV_A_WIDTH - 1 - j))
    return taps


def _conv31_prompt(u_scr, cw_ref, cb_ref, ca_scr, bm):
    def body(blk, carry):
        base = pl.multiple_of(blk * RB, RB)
        for lc in range(D_CONV // LC):
            lanes = pl.ds(lc * LC, LC)
            win = u_scr[pl.ds(base, RB + HIST_A), lanes]
            acc = jnp.broadcast_to(cb_ref[:, lanes], (RB, LC))
            v, v_r = win, 0
            for r, q, k in _conv31_taps():
                if r != v_r:
                    v, v_r = pltpu.roll(win, r, 0), r
                off = HIST_A - SUBLANES * q
                acc = acc + cw_ref[k:k + 1, lanes] * v[off:off + RB, :]
            ca_scr[pl.ds(base, RB), lanes] = acc
        return carry
    lax.fori_loop(0, bm // RB, body, 0)


def _ln_silu_rows(ca_scr, lg_ref, lb_ref, ya_ref, bm):
    def body(r, carry):
        rows = pl.ds(pl.multiple_of(r * RB, RB), RB)
        y = _ln(ca_scr[rows, :], lg_ref[...], lb_ref[...])
        ya_ref[rows, :] = (y * _sigmoid(y)).astype(BF16)
        return carry
    lax.fori_loop(0, bm // RB, body, 0, unroll=2)


def _a1_prompt_kernel(x_ref, wv_ref, wg_ref, cw_ref, cb_ref, lg_ref, lb_ref, ya_ref, na_ref, u_scr, ca_scr, *, bm, tps):
    t = pl.program_id(0) % tps

    @pl.when(t == 0)
    def _():
        u_scr[0:HIST_A, :] = jnp.zeros((HIST_A, D_CONV), F32)

    @pl.when(t != 0)
    def _():
        u_scr[0:HIST_A, :] = u_scr[bm:bm + HIST_A, :]

    xb = x_ref[...].astype(BF16)
    for c in range(D_CONV // CHUNK):
        cols = slice(c * CHUNK, (c + 1) * CHUNK)
        av = _dot(xb, wv_ref[:, cols])
        ag = _dot(xb, wg_ref[:, cols])
        u_scr[HIST_A:HIST_A + bm, cols] = av * _sigmoid(ag)
    na_ref[...] = u_scr[HIST_A + bm - (CONV_A_WIDTH - 1):HIST_A + bm, :]
    _conv31_prompt(u_scr, cw_ref, cb_ref, ca_scr, bm)
    _ln_silu_rows(ca_scr, lg_ref, lb_ref, ya_ref, bm)


def _a1_sample_kernel(x_ref, st_ref, wv_ref, wg_ref, cw_ref, cb_ref, lg_ref, lb_ref, ya_ref, na_ref, ext_scr, ca_scr, *, s_n):
    hist = CONV_A_WIDTH - 1
    xb = x_ref[...].reshape(TS * s_n, D_MODEL).astype(BF16)
    ext_scr[0:hist] = st_ref[...]
    for c in range(D_CONV // CHUNK):
        cols = slice(c * CHUNK, (c + 1) * CHUNK)
        av = _dot(xb, wv_ref[:, cols])
        ag = _dot(xb, wg_ref[:, cols])
        ext_scr[hist:hist + TS, :, cols] = (av * _sigmoid(ag)).reshape(TS, s_n, CHUNK)
    na_ref[...] = ext_scr[TS:TS + hist]

    def conv(t, carry):
        for lc in range(D_CONV // CHUNK):
            lanes = pl.ds(lc * CHUNK, CHUNK)
            acc = jnp.broadcast_to(cb_ref[:, lanes], (s_n, CHUNK))
            for k in range(CONV_A_WIDTH):
                acc = acc + cw_ref[k:k + 1, lanes] * ext_scr[t + k, :, lanes]
            ca_scr[t, :, lanes] = acc
        return carry
    lax.fori_loop(0, TS, conv, 0)

    def norm(t, carry):
        y = _ln(ca_scr[t], lg_ref[...], lb_ref[...])
        ya_ref[t] = (y * _sigmoid(y)).astype(BF16)
        return carry
    lax.fori_loop(0, TS, norm, 0, unroll=2)


def _a2_prompt_kernel(x_ref, wbx_ref, wbg_ref, cw_ref, cb_ref, wr_ref, br_ref, wi_ref, bi_ref, lam_ref,
                      yb_ref, nb_ref, hl_ref, bx_scr, a_scr, u_scr, g_scr, hc_scr, *, bm, tps):
    t = pl.program_id(0) % tps
    hist = CONV_B_WIDTH - 1
    nch = D_LRU // CHUNK

    @pl.when(t == 0)
    def _():
        bx_scr[0:HIST, :] = jnp.zeros((HIST, D_LRU), F32)
        hc_scr[...] = jnp.zeros((SUBLANES, D_LRU), F32)

    @pl.when(t != 0)
    def _():
        bx_scr[0:HIST, :] = bx_scr[bm:bm + HIST, :]

    xb = x_ref[...].astype(BF16)
    first_tile = t == 0
    chunk = lambda c: slice(c * CHUNK, (c + 1) * CHUNK)
    bx_next = _dot(xb, wbx_ref[:, chunk(0)])
    for c in range(nch):
        cols = chunk(c)
        bx = bx_next
        bx_scr[HIST:HIST + bm, cols] = bx
        win = bx_scr[0:HIST + bm, cols]
        cb = cb_ref[:, cols] + cw_ref[hist:hist + 1, cols] * bx
        for k in range(hist):
            cb = cb + cw_ref[k:k + 1, cols] * pltpu.roll(win, hist - k, 0)[HIST:HIST + bm, :]
        r_pre, g_pre = _gate_dots(cb, wr_ref, wi_ref, c)
        if c + 1 < nch:
            bx_next = _dot(xb, wbx_ref[:, chunk(c + 1)])
        bg = _dot(xb, wbg_ref[:, cols])
        a, u = _lru_coeffs(cb, r_pre, g_pre, br_ref, bi_ref, lam_ref, c, first_tile)
        a, u = _scan8(a, u)
        a_scr[:, cols] = a
        u_scr[:, cols] = u
        g_scr[:, cols] = _gelu(bg)
    nb_ref[...] = bx_scr[HIST + bm - hist:HIST + bm, :]

    def body(r, carry):
        rows = pl.ds(pl.multiple_of(r * 2 * SUBLANES, 2 * SUBLANES), 2 * SUBLANES)
        a2, u2 = a_scr[rows, :], u_scr[rows, :]
        h0 = u2[0:SUBLANES, :] + a2[0:SUBLANES, :] * carry
        mid = jnp.broadcast_to(h0[SUBLANES - 1:SUBLANES, :], (SUBLANES, D_LRU))
        h1 = u2[SUBLANES:, :] + a2[SUBLANES:, :] * mid
        yb_ref[rows, :] = (jnp.concatenate([h0, h1], axis=0) * g_scr[rows, :]).astype(BF16)
        return jnp.broadcast_to(h1[SUBLANES - 1:SUBLANES, :], (SUBLANES, D_LRU))
    carry = lax.fori_loop(0, bm // (2 * SUBLANES), body, hc_scr[...])
    hc_scr[...] = carry
    hl_ref[...] = carry[0:1, :]


def _a2_sample_kernel(x_ref, stb_ref, h0_ref, wbx_ref, wbg_ref, cw_ref, cb_ref, wr_ref, br_ref, wi_ref, bi_ref, lam_ref,
                      yb_ref, nb_ref, hl_ref, *, s_n):
    hist = CONV_B_WIDTH - 1
    xb = x_ref[...].reshape(TS * s_n, D_MODEL).astype(BF16)
    nch = D_LRU // CHUNK
    chunk = lambda c: slice(c * CHUNK, (c + 1) * CHUNK)
    slab = lambda v, t: v[t * s_n:(t + 1) * s_n, :]
    bx_next = _dot(xb, wbx_ref[:, chunk(0)])
    for c in range(nch):
        cols = chunk(c)
        bx = bx_next
        ext = [stb_ref[k, :, cols] for k in range(hist)] + [slab(bx, t) for t in range(TS)]
        cb = jnp.concatenate(
            [cb_ref[:, cols] + sum(cw_ref[k:k + 1, cols] * ext[t + k] for k in range(CONV_B_WIDTH)) for t in range(TS)],
            axis=0)
        r_pre, g_pre = _gate_dots(cb, wr_ref, wi_ref, c)
        if c + 1 < nch:
            bx_next = _dot(xb, wbx_ref[:, chunk(c + 1)])
        bg = _dot(xb, wbg_ref[:, cols])
        a, u = _lru_coeffs(cb, r_pre, g_pre, br_ref, bi_ref, lam_ref, c, None)
        h = h0_ref[:, cols]
        hs = []
        for t in range(TS):
            h = slab(a, t) * h + slab(u, t)
            hs.append(h)
        hl_ref[:, cols] = h
        yb_ref[:, :, cols] = (jnp.concatenate(hs, axis=0) * _gelu(bg)).astype(BF16).reshape(TS, s_n, CHUNK)
        for k in range(hist):
            nb_ref[k, :, cols] = slab(bx, TS - hist + k)


def _merge_kernel(x_ref, ya_ref, yb_ref, wga_ref, wgb_ref, wa_ref, wb_ref, wo_ref, g_ref, b_ref, o_ref, acc_scr, *, bm, nc):
    c = pl.program_id(1)
    sb = bm // NSPLIT

    def up(s):
        rows = slice(s * sb, (s + 1) * sb)
        xb = x_ref[rows, :].astype(BF16)
        return (_dot(xb, wga_ref[...]), _dot(ya_ref[rows, :], wa_ref[...]),
                _dot(xb, wgb_ref[...]), _dot(yb_ref[rows, :], wb_ref[...]))

    def down(s, last, ga, out_a, gb, out_b):
        rows = slice(s * sb, (s + 1) * sb)
        merged = _sigmoid(ga) * out_a + _sigmoid(gb) * out_b
        d = _dot(merged.astype(BF16), wo_ref[...])
        if last == "first":
            acc_scr[rows, :] = d
        elif last:
            o_ref[rows, :] = _ln(ALPHA * x_ref[rows, :] + (acc_scr[rows, :] + d), g_ref[...], b_ref[...])
        else:
            acc_scr[rows, :] += d

    def body(last):
        pending = up(0)
        for s in range(NSPLIT):
            nxt = up(s + 1) if s + 1 < NSPLIT else None
            down(s, last, *pending)
            pending = nxt

    @pl.when(c == 0)
    def _():
        body("first")

    @pl.when(jnp.logical_and(c != 0, c != nc - 1))
    def _():
        body(False)

    @pl.when(c == nc - 1)
    def _():
        body(True)


def _ffn_prompt_kernel(x_ref, wfu_ref, wfg_ref, cw_ref, cb_ref, wd_ref, g_ref, b_ref, o_ref, nf_ref,
                       acc_scr, fg_scr, carry_scr, *, bm, tps, nc):
    t = pl.program_id(0) % tps
    c = pl.program_id(1)
    hist = FFN_CONV_WIDTH - 1

    @pl.when(t == 0)
    def _():
        fg_scr[0:HIST, :] = jnp.zeros((HIST, CF), F32)

    @pl.when(t != 0)
    def _():
        fg_scr[0:HIST, :] = carry_scr[c]

    sb = bm // NSPLIT

    def up(s):
        xb = x_ref[s * sb:(s + 1) * sb, :].astype(BF16)
        fg = _dot(xb, wfg_ref[...])
        fg_scr[HIST + s * sb:HIST + (s + 1) * sb, :] = fg
        return _dot(xb, wfu_ref[...]), fg

    def down(s, last, fu, fg):
        r0 = s * sb
        fgc = cb_ref[...] + cw_ref[hist:hist + 1, :] * fg
        for k in range(hist):
            fgc = fgc + cw_ref[k:k + 1, :] * fg_scr[HIST - hist + k + r0:HIST - hist + k + r0 + sb, :]
        hf = _gelu(fgc) * fu
        d = _dot(hf.astype(BF16), wd_ref[...])
        if last == "first":
            acc_scr[r0:r0 + sb, :] = d
        elif last:
            o_ref[r0:r0 + sb, :] = _ln(ALPHA * x_ref[r0:r0 + sb, :] + (acc_scr[r0:r0 + sb, :] + d), g_ref[...], b_ref[...])
        else:
            acc_scr[r0:r0 + sb, :] += d

    def body(last):
        pending = up(0)
        for s in range(NSPLIT):
            nxt = up(s + 1) if s + 1 < NSPLIT else None
            down(s, last, *pending)
            pending = nxt

    @pl.when(c == 0)
    def _():
        body("first")

    @pl.when(jnp.logical_and(c != 0, c != nc - 1))
    def _():
        body(False)

    @pl.when(c == nc - 1)
    def _():
        body(True)

    carry_scr[c] = fg_scr[bm:bm + HIST, :]
    nf_ref[...] = fg_scr[HIST + bm - hist:HIST + bm, :]


def _ffn_sample_kernel(x_ref, st_ref, wfu_ref, wfg_ref, cw_ref, cb_ref, wd_ref, g_ref, b_ref,
                       o_ref, nf_ref, wfub_ref, wfgb_ref, wdb_ref, xb_scr, *, s_n, nc):
    c = pl.program_id(0)
    hist = FFN_CONV_WIDTH - 1
    rows = TS * s_n

    @pl.when(c == 0)
    def _():
        xb_scr[...] = x_ref[...].astype(BF16)
        o_ref[...] = ALPHA * x_ref[...]

    wfub_ref[...] = wfu_ref[...].astype(BF16)
    wfgb_ref[...] = wfg_ref[...].astype(BF16)
    wdb_ref[...] = wd_ref[...].astype(BF16)

    spt = TS // NSPLIT_S
    sb = spt * s_n

    def up(j):
        xb = xb_scr[j * sb:(j + 1) * sb, :]
        return _dot(xb, wfub_ref[...]), _dot(xb, wfgb_ref[...])

    gate = [st_ref[k] for k in range(hist)]
    pending = up(0)
    for j in range(NSPLIT_S):
        nxt = up(j + 1) if j + 1 < NSPLIT_S else None
        fu, fg = pending
        gate += [fg[q * s_n:(q + 1) * s_n, :] for q in range(spt)]
        fgc = jnp.concatenate(
            [cb_ref[...] + sum(cw_ref[k:k + 1, :] * gate[j * spt + q + k] for k in range(FFN_CONV_WIDTH))
             for q in range(spt)], axis=0)
        hf = _gelu(fgc) * fu
        o_ref[j * sb:(j + 1) * sb, :] += _dot(hf.astype(BF16), wdb_ref[...])
        pending = nxt
    for k in range(hist):
        nf_ref[k] = gate[TS + k]

    @pl.when(c == nc - 1)
    def _():
        _residual_ln_rows(None, o_ref, g_ref, b_ref, o_ref, rows)


def _ple_kernel(x_ref, p_ref, wpg_ref, bpg_ref, wpe_ref, g_ref, b_ref, o_ref, e_scr, *, bm):
    sb = bm // NSPLIT

    def embed(s):
        rows = slice(s * sb, (s + 1) * sb)
        xb = x_ref[rows, :].astype(BF16)
        pb = p_ref[rows, :].astype(BF16)
        for c in range(D_MODEL // CHUNK):
            cols = slice(c * CHUNK, (c + 1) * CHUNK)
            gate = _sigmoid(_dot(xb, wpg_ref[:, cols]) + bpg_ref[:, cols])
            e_scr[rows, cols] = gate * _dot(pb, wpe_ref[:, cols])

    embed(0)
    for s in range(NSPLIT):
        if s + 1 < NSPLIT:
            embed(s + 1)
        rows = slice(s * sb, (s + 1) * sb)
        o_ref[rows, :] = _ln(ALPHA * x_ref[rows, :] + e_scr[rows, :], g_ref[...], b_ref[...])


def _params(n_axes):
    return pltpu.CompilerParams(dimension_semantics=("arbitrary",) * n_axes, vmem_limit_bytes=VMEM_LIMIT_BYTES)


def _resident(shape, index_map):
    return pl.BlockSpec(shape, index_map, pipeline_mode=pl.Buffered(1))


def _row(l, width):
    return _resident((None, 1, width), lambda *idx: (l, 0, 0))


def _mixer_a(l, x, state, w_in_b, conv_w, conv_b, ln_g, ln_b, prompt, seqs):
    hist = CONV_A_WIDTH - 1
    weights = [
        _resident((None, D_MODEL, D_CONV), lambda i: (l, 0, OFF_AV // D_CONV)),
        _resident((None, D_MODEL, D_CONV), lambda i: (l, 0, OFF_AG // D_CONV)),
        _resident((None, CONV_A_WIDTH, D_CONV), lambda i: (l, 0, 0)),
        _row(l, D_CONV), _row(l, D_CONV), _row(l, D_CONV),
    ]
    args = (w_in_b, w_in_b, conv_w, conv_b, ln_g, ln_b)
    if prompt:
        n, bm = x.shape[0], BM_PROMPT
        tps = (n // seqs) // bm
        return pl.pallas_call(
            functools.partial(_a1_prompt_kernel, bm=bm, tps=tps),
            grid=(n // bm,),
            in_specs=[pl.BlockSpec((bm, D_MODEL), lambda i: (i, 0))] + weights,
            out_specs=[pl.BlockSpec((bm, D_CONV), lambda i: (i, 0)),
                       pl.BlockSpec((None, hist, D_CONV), lambda i: (i // tps, 0, 0))],
            out_shape=[jax.ShapeDtypeStruct((n, D_CONV), BF16), jax.ShapeDtypeStruct((seqs, hist, D_CONV), F32)],
            scratch_shapes=[pltpu.VMEM((HIST_A + bm, D_CONV), F32), pltpu.VMEM((bm, D_CONV), F32)],
            compiler_params=_params(1), name=f"mixer_a_prompt_l{l}",
        )(x, *args)
    s_n = S_MIX_A
    return pl.pallas_call(
        functools.partial(_a1_sample_kernel, s_n=s_n),
        grid=(seqs // s_n,),
        in_specs=[pl.BlockSpec((TS, s_n, D_MODEL), lambda i: (0, i, 0)),
                  pl.BlockSpec((None, hist, s_n, D_CONV), lambda i: (l, 0, i, 0))] + weights,
        out_specs=[pl.BlockSpec((TS, s_n, D_CONV), lambda i: (0, i, 0)),
                   pl.BlockSpec((hist, s_n, D_CONV), lambda i: (0, i, 0))],
        out_shape=[jax.ShapeDtypeStruct((TS, seqs, D_CONV), BF16), jax.ShapeDtypeStruct((hist, seqs, D_CONV), F32)],
        scratch_shapes=[pltpu.VMEM((hist + TS, s_n, D_CONV), F32), pltpu.VMEM((TS, s_n, D_CONV), F32)],
        compiler_params=_params(1), name=f"mixer_a_sample_l{l}",
    )(x, state, *args)


def _mixer_b(l, x, state_b, state_h, w_in_b, conv_w, conv_b, w_r_b, b_r, w_i_b, b_i, lam, prompt, seqs):
    bm = BM_MIXER_B
    hist = CONV_B_WIDTH - 1
    gate_w = _resident((None, LRU_HEADS, LRU_HEAD_DIM, LRU_HEAD_DIM), lambda i: (l, 0, 0, 0))
    weights = [
        _resident((None, D_MODEL, D_LRU), lambda i: (l, 0, OFF_BX // D_LRU)),
        _resident((None, D_MODEL, D_LRU), lambda i: (l, 0, OFF_BG // D_LRU)),
        _resident((None, CONV_B_WIDTH, D_LRU), lambda i: (l, 0, 0)),
        _row(l, D_LRU), gate_w, _row(l, D_LRU), gate_w, _row(l, D_LRU), _row(l, D_LRU),
    ]
    args = (w_in_b, w_in_b, conv_w, conv_b, w_r_b, b_r, w_i_b, b_i, lam)
    if prompt:
        n = x.shape[0]
        tps = (n // seqs) // bm
        yb, nb, hl = pl.pallas_call(
            functools.partial(_a2_prompt_kernel, bm=bm, tps=tps),
            grid=(n // bm,),
            in_specs=[pl.BlockSpec((bm, D_MODEL), lambda i: (i, 0))] + weights,
            out_specs=[pl.BlockSpec((bm, D_LRU), lambda i: (i, 0)),
                       pl.BlockSpec((None, hist, D_LRU), lambda i: (i // tps, 0, 0)),
                       pl.BlockSpec((None, 1, D_LRU), lambda i: (i // tps, 0, 0))],
            out_shape=[jax.ShapeDtypeStruct((n, D_LRU), BF16),
                       jax.ShapeDtypeStruct((seqs, hist, D_LRU), F32),
                       jax.ShapeDtypeStruct((seqs, 1, D_LRU), F32)],
            scratch_shapes=[pltpu.VMEM((HIST + bm, D_LRU), F32), pltpu.VMEM((bm, D_LRU), F32),
                            pltpu.VMEM((bm, D_LRU), F32), pltpu.VMEM((bm, D_LRU), F32),
                            pltpu.VMEM((SUBLANES, D_LRU), F32)],
            compiler_params=_params(1), name=f"mixer_b_prompt_l{l}",
        )(x, *args)
        return yb, nb, hl.reshape(seqs, D_LRU)
    s_n = S_MIX_B
    return pl.pallas_call(
        functools.partial(_a2_sample_kernel, s_n=s_n),
        grid=(seqs // s_n,),
        in_specs=[pl.BlockSpec((TS, s_n, D_MODEL), lambda i: (0, i, 0)),
                  pl.BlockSpec((None, hist, s_n, D_LRU), lambda i: (l, 0, i, 0)),
                  pl.BlockSpec((None, s_n, D_LRU), lambda i: (l, i, 0))] + weights,
        out_specs=[pl.BlockSpec((TS, s_n, D_LRU), lambda i: (0, i, 0)),
                   pl.BlockSpec((hist, s_n, D_LRU), lambda i: (0, i, 0)),
                   pl.BlockSpec((s_n, D_LRU), lambda i: (i, 0))],
        out_shape=[jax.ShapeDtypeStruct((TS, seqs, D_LRU), BF16),
                   jax.ShapeDtypeStruct((hist, seqs, D_LRU), F32),
                   jax.ShapeDtypeStruct((seqs, D_LRU), F32)],
        compiler_params=_params(1), name=f"mixer_b_sample_l{l}",
    )(x, state_b, state_h, *args)


def _merge(l, x2, ya, yb, w_in_b, w_a_out_b, w_b_out_b, w_o_b, g, b, prompt):
    n = x2.shape[0]
    bm = BM_PROMPT if prompt else BM_SAMPLE
    nc = D_MODEL // CM
    return pl.pallas_call(
        functools.partial(_merge_kernel, bm=bm, nc=nc),
        grid=(n // bm, nc),
        in_specs=[
            pl.BlockSpec((bm, D_MODEL), lambda i, c: (i, 0)),
            pl.BlockSpec((bm, D_CONV), lambda i, c: (i, 0)),
            pl.BlockSpec((bm, D_LRU), lambda i, c: (i, 0)),
            pl.BlockSpec((None, D_MODEL, CM), lambda i, c: (l, 0, OFF_GA // CM + c)),
            pl.BlockSpec((None, D_MODEL, CM), lambda i, c: (l, 0, OFF_GB // CM + c)),
            pl.BlockSpec((None, D_CONV, CM), lambda i, c: (l, 0, c)),
            pl.BlockSpec((None, D_LRU, CM), lambda i, c: (l, 0, c)),
            pl.BlockSpec((None, CM, D_MODEL), lambda i, c: (l, c, 0)),
            _row(l, D_MODEL), _row(l, D_MODEL),
        ],
        out_specs=pl.BlockSpec((bm, D_MODEL), lambda i, c: (i, 0)),
        out_shape=jax.ShapeDtypeStruct((n, D_MODEL), F32),
        scratch_shapes=[pltpu.VMEM((bm, D_MODEL), F32)],
        compiler_params=_params(2), name=f"merge_{'prompt' if prompt else 'sample'}_l{l}",
    )(x2, ya, yb, w_in_b, w_in_b, w_a_out_b, w_b_out_b, w_o_b, g, b)


def _ffn_sample(l, x2, state, w_up, conv_w, conv_b, w_down, g, b, seqs):
    n = x2.shape[0]
    nc = D_FF // CF_S
    hist = FFN_CONV_WIDTH - 1
    return pl.pallas_call(
        functools.partial(_ffn_sample_kernel, s_n=seqs, nc=nc),
        grid=(nc,),
        in_specs=[
            _resident((n, D_MODEL), lambda c: (0, 0)),
            pl.BlockSpec((None, hist, seqs, CF_S), lambda c: (l, 0, 0, c)),
            pl.BlockSpec((None, D_MODEL, CF_S), lambda c: (l, 0, c)),
            pl.BlockSpec((None, D_MODEL, CF_S), lambda c: (l, 0, nc + c)),
            pl.BlockSpec((None, FFN_CONV_WIDTH, CF_S), lambda c: (l, 0, c)),
            pl.BlockSpec((None, 1, CF_S), lambda c: (l, 0, c)),
            pl.BlockSpec((None, CF_S, D_MODEL), lambda c: (l, c, 0)),
            _row(l, D_MODEL), _row(l, D_MODEL),
        ],
        out_specs=[
            pl.BlockSpec((n, D_MODEL), lambda c: (0, 0)),
            pl.BlockSpec((hist, seqs, CF_S), lambda c: (0, 0, c)),
            pl.BlockSpec((D_MODEL, CF_S), lambda c: (0, c)),
            pl.BlockSpec((D_MODEL, CF_S), lambda c: (0, c)),
            pl.BlockSpec((CF_S, D_MODEL), lambda c: (c, 0)),
        ],
        out_shape=[
            jax.ShapeDtypeStruct((n, D_MODEL), F32),
            jax.ShapeDtypeStruct((hist, seqs, D_FF), F32),
            jax.ShapeDtypeStruct((D_MODEL, D_FF), BF16),
            jax.ShapeDtypeStruct((D_MODEL, D_FF), BF16),
            jax.ShapeDtypeStruct((D_FF, D_MODEL), BF16),
        ],
        scratch_shapes=[pltpu.VMEM((n, D_MODEL), BF16)],
        compiler_params=_params(1), name=f"ffn_sample_l{l}",
    )(x2, state, w_up, w_up, conv_w, conv_b, w_down, g, b)


def _ffn_prompt(l, x2, wfu_b, wfg_b, conv_w, conv_b, wd_b, g, b, seqs):
    n = x2.shape[0]
    bm = BM_PROMPT
    nc = D_FF // CF
    hist = FFN_CONV_WIDTH - 1
    tps = (n // seqs) // bm
    x_out, nf_tiles = pl.pallas_call(
        functools.partial(_ffn_prompt_kernel, bm=bm, tps=tps, nc=nc),
        grid=(n // bm, nc),
        in_specs=[
            pl.BlockSpec((bm, D_MODEL), lambda i, c: (i, 0)),
            pl.BlockSpec((D_MODEL, CF), lambda i, c: (0, c)),
            pl.BlockSpec((D_MODEL, CF), lambda i, c: (0, c)),
            pl.BlockSpec((None, FFN_CONV_WIDTH, CF), lambda i, c: (l, 0, c)),
            pl.BlockSpec((None, 1, CF), lambda i, c: (l, 0, c)),
            pl.BlockSpec((CF, D_MODEL), lambda i, c: (c, 0)),
            _row(l, D_MODEL), _row(l, D_MODEL),
        ],
        out_specs=[pl.BlockSpec((bm, D_MODEL), lambda i, c: (i, 0)),
                   pl.BlockSpec((None, hist, CF), lambda i, c: (i, 0, c))],
        out_shape=[jax.ShapeDtypeStruct((n, D_MODEL), F32), jax.ShapeDtypeStruct((n // bm, hist, D_FF), F32)],
        scratch_shapes=[pltpu.VMEM((bm, D_MODEL), F32),
                        pltpu.VMEM((HIST + bm, CF), F32), pltpu.VMEM((nc, HIST, CF), F32)],
        compiler_params=_params(2), name=f"ffn_prompt_l{l}",
    )(x2, wfu_b, wfg_b, conv_w, conv_b, wd_b, g, b)
    return x_out, nf_tiles.reshape(seqs, tps, hist, D_FF)[:, tps - 1]


def _ple(l, x2, p2, w_pg_b, b_pg, w_pe_b, g, b, prompt):
    n = x2.shape[0]
    bm = BM_PROMPT if prompt else BM_SAMPLE
    return pl.pallas_call(
        functools.partial(_ple_kernel, bm=bm),
        grid=(n // bm,),
        in_specs=[
            pl.BlockSpec((bm, D_MODEL), lambda i: (i, 0)),
            pl.BlockSpec((None, bm, D_PLE), lambda i: (l, i, 0)),
            _resident((None, D_MODEL, D_MODEL), lambda i: (l, 0, 0)),
            _row(l, D_MODEL),
            _resident((None, D_PLE, D_MODEL), lambda i: (l, 0, 0)),
            _row(l, D_MODEL), _row(l, D_MODEL),
        ],
        out_specs=pl.BlockSpec((bm, D_MODEL), lambda i: (i, 0)),
        out_shape=jax.ShapeDtypeStruct((n, D_MODEL), F32),
        scratch_shapes=[pltpu.VMEM((bm, D_MODEL), F32)],
        compiler_params=_params(1), name=f"ple_{'prompt' if prompt else 'sample'}_l{l}",
    )(x2, p2, w_pg_b, b_pg, w_pe_b, g, b)


def kernel(x_prompt, x_sample, state_conv_a, state_conv_b, state_rglru, state_conv_ffn, p_prompt, p_sample, w_in, conv_a_w, conv_a_b, ln_a_g, ln_a_b, w_a_out, conv_b_w, conv_b_b, w_r, b_r, w_i, b_i, lru_lambda, w_b_out, w_o, ln1_g, ln1_b, w_up, ffn_conv_w, ffn_conv_b, w_down, ln2_g, ln2_b, w_pe, w_pg, b_pg, ln3_g, ln3_b):
    bp, tp, _ = x_prompt.shape
    bs, ts, _ = x_sample.shape
    assert ts == TS and tp % BM_PROMPT == 0 and tp % BM_MIXER_B == 0
    assert bs % S_MIX_A == 0 and bs % S_MIX_B == 0 and (bs * ts) % BM_SAMPLE == 0 and TS % NSPLIT_S == 0
    assert w_in.shape == (DEPTH, D_MODEL, OFF_GB + D_MODEL)

    w_in_b, w_a_out_b, w_b_out_b, w_o_b = (w.astype(BF16) for w in (w_in, w_a_out, w_b_out, w_o))
    w_pe_b, w_pg_b, w_r_b, w_i_b = (w.astype(BF16) for w in (w_pe, w_pg, w_r, w_i))
    row = lambda v: v.reshape(DEPTH, 1, v.shape[-1])
    conv_a_b, ln_a_g, ln_a_b, conv_b_b, b_r, b_i, lru_lambda = map(row, (conv_a_b, ln_a_g, ln_a_b, conv_b_b, b_r, b_i, lru_lambda))
    ln1_g, ln1_b, ffn_conv_b, ln2_g, ln2_b, b_pg, ln3_g, ln3_b = map(row, (ln1_g, ln1_b, ffn_conv_b, ln2_g, ln2_b, b_pg, ln3_g, ln3_b))

    xp = x_prompt.reshape(bp * tp, D_MODEL)
    pp = p_prompt.reshape(DEPTH, bp * tp, D_PLE)
    xs = jnp.transpose(x_sample, (1, 0, 2))
    ps = jnp.transpose(p_sample, (0, 2, 1, 3)).reshape(DEPTH, ts * bs, D_PLE)
    sa, sb, sf = (jnp.transpose(s, (0, 2, 1, 3)) for s in (state_conv_a, state_conv_b, state_conv_ffn))
    flat = lambda v: v.reshape(ts * bs, v.shape[-1])

    prompt_states, sample_states = [], []
    for l in range(DEPTH):
        ya, new_a = _mixer_a(l, xs, sa, w_in_b, conv_a_w, conv_a_b, ln_a_g, ln_a_b, False, bs)
        yb, new_b, h_last = _mixer_b(l, xs, sb, state_rglru, w_in_b, conv_b_w, conv_b_b,
                                     w_r_b, b_r, w_i_b, b_i, lru_lambda, False, bs)
        x2 = _merge(l, flat(xs), flat(ya), flat(yb), w_in_b, w_a_out_b, w_b_out_b, w_o_b, ln1_g, ln1_b, False)
        x2, new_f, wfu_b, wfg_b, wd_b = _ffn_sample(l, x2, sf, w_up, ffn_conv_w, ffn_conv_b, w_down, ln2_g, ln2_b, bs)
        xs = _ple(l, x2, ps, w_pg_b, b_pg, w_pe_b, ln3_g, ln3_b, False).reshape(ts, bs, D_MODEL)
        sample_states.append((new_a, new_b, h_last, new_f))

        ya, new_a = _mixer_a(l, xp, None, w_in_b, conv_a_w, conv_a_b, ln_a_g, ln_a_b, True, bp)
        yb, new_b, h_last = _mixer_b(l, xp, None, None, w_in_b, conv_b_w, conv_b_b,
                                     w_r_b, b_r, w_i_b, b_i, lru_lambda, True, bp)
        xp = _merge(l, xp, ya, yb, w_in_b, w_a_out_b, w_b_out_b, w_o_b, ln1_g, ln1_b, True)
        xp, new_f = _ffn_prompt(l, xp, wfu_b, wfg_b, ffn_conv_w, ffn_conv_b, wd_b, ln2_g, ln2_b, bp)
        xp = _ple(l, xp, pp, w_pg_b, b_pg, w_pe_b, ln3_g, ln3_b, True)
        prompt_states.append((new_a, new_b, h_last, new_f))

    stack = lambda states, k: jnp.stack([s[k] for s in states])
    seq_major = lambda v: jnp.transpose(v, (0, 2, 1, 3))
    return (xp.reshape(bp, tp, D_MODEL), jnp.transpose(xs, (1, 0, 2)),
            stack(prompt_states, 0), stack(prompt_states, 1), stack(prompt_states, 2), stack(prompt_states, 3),
            seq_major(stack(sample_states, 0)), seq_major(stack(sample_states, 1)), stack(sample_states, 2),
            seq_major(stack(sample_states, 3)))
```
